```python
import math
import jax, jax.numpy as jnp
from jax import lax
import numpy as np


D_MODEL = 1024
BATCH = 2
SEQ = 8192
DEPTH = 2
DEC_BATCH = 128
DEC_SEQ = 8
PAST_LEN = 2048
PAGE_SIZE = 128

N_A_LAYERS = DEPTH // 2
N_B_LAYERS = DEPTH - N_A_LAYERS
SSM_GROUP = 16
N_GROUPS = D_MODEL // SSM_GROUP
SSM_STATE = 64
SSM_CHUNK = 128
DT_MIN = 1e-3
DT_MAX = 1e-1
N_HEADS = 8
HEAD_DIM = 64
V_DIM = 2 * HEAD_DIM
QK_WIDTH = N_HEADS * 2 * HEAD_DIM
V_WIDTH = N_HEADS * V_DIM
D_FF = 2816
CONV_W = 3
Q_BLOCK = 128
NORM_EPS = 1e-6

kernel_name = 'yoco_s5_diffattn_convffn_step'


def rmsnorm(x, g):
    xf = x.astype(jnp.float32)
    y = xf * lax.rsqrt(jnp.mean(xf * xf, axis=-1, keepdims=True) + NORM_EPS)
    return (y * g.astype(jnp.float32)).astype(x.dtype)


def _ssm_combine(e1, e2):
    a1, b1 = e1
    a2, b2 = e2
    return a1 * a2, a2 * b1 + b2


def s5_mixer(u, h0, lam_re, lam_im, log_dt, b_re, b_im, c_re, c_im, d_skip, w_glu):
    bsz, seq_len, _ = u.shape
    lam = lax.complex(lam_re.astype(jnp.float32), lam_im.astype(jnp.float32))
    dt = jnp.exp(log_dt.astype(jnp.float32))[:, None]
    lam_dt = lam * dt
    lam_bar = jnp.exp(lam_dt)
    b = lax.complex(b_re.astype(jnp.float32), b_im.astype(jnp.float32))
    b_bar = ((lam_bar - 1.0) / lam)[..., None] * b
    c = lax.complex(c_re.astype(jnp.float32), c_im.astype(jnp.float32))
    chunk = SSM_CHUNK if seq_len % SSM_CHUNK == 0 else seq_len
    n_chunks = seq_len // chunk
    steps = jnp.arange(1, chunk + 1, dtype=jnp.float32)[:, None, None]
    pows = jnp.exp(lam_dt[None] * steps)
    uf = u.astype(jnp.float32).reshape(bsz, n_chunks, chunk, N_GROUPS, SSM_GROUP)
    u_chunks = jnp.moveaxis(uf, 1, 0)

    def step(h_prev, u_c):
        bu = jnp.einsum('blgc,gpc->blgp', u_c.astype(jnp.complex64), b_bar)
        a = jnp.broadcast_to(lam_bar, bu.shape)
        _, h_zero = lax.associative_scan(_ssm_combine, (a, bu), axis=1)
        h = pows[None] * h_prev[:, None] + h_zero
        y_c = jnp.real(jnp.einsum('blgp,gcp->blgc', h, c))
        return h[:, -1], y_c

    h_last, ys = lax.scan(step, h0, u_chunks)
    y = jnp.moveaxis(ys, 0, 1).reshape(bsz, seq_len, D_MODEL)
    y = y + d_skip.astype(jnp.float32) * u.astype(jnp.float32)
    z = jax.nn.gelu(y).astype(u.dtype)
    gl = z @ w_glu
    out = gl[..., :D_MODEL] * jax.nn.sigmoid(gl[..., D_MODEL:])
    return out, h_last


def conv_ffn(x, buf, w_up, conv_w, conv_b, w_down):
    seq_len = x.shape[1]
    h = x @ w_up
    hp = jnp.concatenate([buf.astype(h.dtype), h], axis=1)
    hc = conv_b + conv_w[0] * hp[:, 0:seq_len]
    for tap in range(1, CONV_W):
        hc = hc + conv_w[tap] * hp[:, tap:tap + seq_len]
    gate, val = jnp.split(hc, 2, axis=-1)
    out = (jax.nn.gelu(gate) * val) @ w_down
    return out, hp[:, -(CONV_W - 1):]


def diff_attention(q, k, v, q_offset, lam, sub_g, lam_init):
    bsz, seq_len = q.shape[:2]
    n_keys = k.shape[1]
    qb = Q_BLOCK if seq_len % Q_BLOCK == 0 else seq_len
    nb = seq_len // qb
    slopes = 2.0 ** (-8.0 * jnp.arange(1, N_HEADS + 1, dtype=jnp.float32) / N_HEADS)
    k_pos = jnp.arange(n_keys, dtype=jnp.int32)
    scale = HEAD_DIM ** -0.5
    q_blocks = jnp.moveaxis(q.reshape(bsz, nb, qb, N_HEADS, 2, HEAD_DIM), 1, 0)

    def block(args):
        i, q_blk = args
        q_pos = q_offset + i * qb + jnp.arange(qb, dtype=jnp.int32)
        s = jnp.einsum('bqhjd,bkhjd->bhjqk', q_blk, k, preferred_element_type=jnp.float32) * scale
        dist = (q_pos[:, None] - k_pos[None, :]).astype(jnp.float32)
        s = s - slopes[:, None, None, None] * dist
        s = jnp.where(k_pos[None, :] <= q_pos[:, None], s, -jnp.inf)
        p = jax.nn.softmax(s, axis=-1)
        w = p[:, :, 0] - lam * p[:, :, 1]
        return jnp.einsum('bhqk,bkhe->bqhe', w.astype(v.dtype), v)

    o = lax.map(block, (jnp.arange(nb, dtype=jnp.int32), q_blocks))
    o = jnp.moveaxis(o, 0, 1).reshape(bsz, seq_len, N_HEADS, V_DIM)
    o = rmsnorm(o, sub_g) * (1.0 - lam_init)
    return o.reshape(bsz, seq_len, V_WIDTH)


def trunk(x, ssm_h0, conv_buf0, past_k, past_v, p):
    bsz, seq_len, _ = x.shape
    ssm_out = []
    conv_out = []
    k_all = v_all = k_new = v_new = None
    for i in range(DEPTH):
        if i < N_A_LAYERS:
            a_out, h_last = s5_mixer(rmsnorm(x, p['a_pre_g'][i]), ssm_h0[i], p['ssm_lam_re'][i], p['ssm_lam_im'][i],
                                     p['ssm_log_dt'][i], p['ssm_b_re'][i], p['ssm_b_im'][i], p['ssm_c_re'][i],
                                     p['ssm_c_im'][i], p['ssm_d'][i], p['glu_w'][i])
            x = x + rmsnorm(a_out, p['a_post_g'][i])
            ssm_out.append(h_last)
        else:
            j = i - N_A_LAYERS
            if j == 0:
                kv_in = rmsnorm(x, p['kv_norm_g'])
                k_new = (kv_in @ p['w_k']).reshape(bsz, seq_len, N_HEADS, 2 * HEAD_DIM)
                v_new = (kv_in @ p['w_v']).reshape(bsz, seq_len, N_HEADS, V_DIM)
                if past_k is None:
                    k_all, v_all = k_new, v_new
                else:
                    k_all = jnp.concatenate([past_k.astype(k_new.dtype), k_new], axis=1)
                    v_all = jnp.concatenate([past_v.astype(v_new.dtype), v_new], axis=1)
            n_keys = k_all.shape[1]
            lam_init = 0.8 - 0.6 * math.exp(-0.3 * i)
            lam = (jnp.exp(jnp.sum(p['lam_q1'][j].astype(jnp.float32) * p['lam_k1'][j].astype(jnp.float32)))
                   - jnp.exp(jnp.sum(p['lam_q2'][j].astype(jnp.float32) * p['lam_k2'][j].astype(jnp.float32)))
                   + lam_init)
            xn = rmsnorm(x, p['b_pre_g'][j])
            q = (xn @ p['w_q'][j]).reshape(bsz, seq_len, N_HEADS, 2, HEAD_DIM)
            o = diff_attention(q, k_all.reshape(bsz, n_keys, N_HEADS, 2, HEAD_DIM), v_all,
                               n_keys - seq_len, lam, p['sub_g'][j], lam_init)
            x = x + rmsnorm(o @ p['w_o'][j], p['b_post_g'][j])
        f_out, buf = conv_ffn(rmsnorm(x, p['f_pre_g'][i]), conv_buf0[i], p['w_up'][i], p['conv_w'][i],
                              p['conv_b'][i], p['w_down'][i])
        x = x + rmsnorm(f_out, p['f_post_g'][i])
        conv_out.append(buf)
    return x, jnp.stack(ssm_out), jnp.stack(conv_out), k_new, v_new


def setup_inputs(seed: int = 0) -> dict:
    key = jax.random.key(seed)
    ks = iter(jax.random.split(key, 48))

    def nrm(shape, scale):
        return jax.random.normal(next(ks), shape, jnp.float32) * scale

    def gain(shape):
        return 1.0 + 0.02 * jax.random.normal(next(ks), shape, jnp.float32)

    n_pages = PAST_LEN // PAGE_SIZE
    n_used = DEC_BATCH * n_pages
    n_pool = n_used + max(1, n_used // 4)
    F2 = 2 * D_FF
    return {
        'x_prompt': nrm((BATCH, SEQ, D_MODEL), 1.0),
        'x_sample': nrm((DEC_BATCH, DEC_SEQ, D_MODEL), 1.0),
        'state_ssm_re': nrm((N_A_LAYERS, DEC_BATCH, N_GROUPS, SSM_STATE), 0.1),
        'state_ssm_im': nrm((N_A_LAYERS, DEC_BATCH, N_GROUPS, SSM_STATE), 0.1),
        'state_conv': nrm((DEPTH, DEC_BATCH, CONV_W - 1, F2), 1.0),
        'cache_k': nrm((n_pool, PAGE_SIZE, N_HEADS, 2 * HEAD_DIM), 1.0),
        'cache_v': nrm((n_pool, PAGE_SIZE, N_HEADS, V_DIM), 1.0),
        'page_table': jax.random.permutation(next(ks), n_pool)[:n_used].reshape(DEC_BATCH, n_pages).astype(jnp.int32),
        'a_pre_g': gain((N_A_LAYERS, D_MODEL)),
        'a_post_g': gain((N_A_LAYERS, D_MODEL)),
        'ssm_lam_re': -0.5 + nrm((N_A_LAYERS, N_GROUPS, SSM_STATE), 0.01),
        'ssm_lam_im': jnp.pi * jnp.arange(SSM_STATE, dtype=jnp.float32) + nrm((N_A_LAYERS, N_GROUPS, SSM_STATE), 0.01),
        'ssm_log_dt': jax.random.uniform(next(ks), (N_A_LAYERS, N_GROUPS), jnp.float32,
                                         minval=math.log(DT_MIN), maxval=math.log(DT_MAX)),
        'ssm_b_re': nrm((N_A_LAYERS, N_GROUPS, SSM_STATE, SSM_GROUP), (2 * SSM_GROUP) ** -0.5),
        'ssm_b_im': nrm((N_A_LAYERS, N_GROUPS, SSM_STATE, SSM_GROUP), (2 * SSM_GROUP) ** -0.5),
        'ssm_c_re': nrm((N_A_LAYERS, N_GROUPS, SSM_GROUP, SSM_STATE), (2 * SSM_STATE) ** -0.5),
        'ssm_c_im': nrm((N_A_LAYERS, N_GROUPS, SSM_GROUP, SSM_STATE), (2 * SSM_STATE) ** -0.5),
        'ssm_d': nrm((N_A_LAYERS, D_MODEL), 1.0),
        'glu_w': nrm((N_A_LAYERS, D_MODEL, 2 * D_MODEL), D_MODEL ** -0.5),
        'kv_norm_g': gain((D_MODEL,)),
        'w_k': nrm((D_MODEL, QK_WIDTH), D_MODEL ** -0.5),
        'w_v': nrm((D_MODEL, V_WIDTH), D_MODEL ** -0.5),
        'b_pre_g': gain((N_B_LAYERS, D_MODEL)),
        'b_post_g': gain((N_B_LAYERS, D_MODEL)),
        'w_q': nrm((N_B_LAYERS, D_MODEL, QK_WIDTH), D_MODEL ** -0.5),
        'lam_q1': nrm((N_B_LAYERS, HEAD_DIM), 0.1),
        'lam_k1': nrm((N_B_LAYERS, HEAD_DIM), 0.1),
        'lam_q2': nrm((N_B_LAYERS, HEAD_DIM), 0.1),
        'lam_k2': nrm((N_B_LAYERS, HEAD_DIM), 0.1),
        'sub_g': gain((N_B_LAYERS, V_DIM)),
        'w_o': nrm((N_B_LAYERS, V_WIDTH, D_MODEL), V_WIDTH ** -0.5),
        'f_pre_g': gain((DEPTH, D_MODEL)),
        'f_post_g': gain((DEPTH, D_MODEL)),
        'w_up': nrm((DEPTH, D_MODEL, F2), D_MODEL ** -0.5),
        'conv_w': nrm((DEPTH, CONV_W, F2), CONV_W ** -0.5),
        'conv_b': nrm((DEPTH, F2), 0.01),
        'w_down': nrm((DEPTH, D_FF, D_MODEL), D_FF ** -0.5),
    }


def reference(x_prompt, x_sample, state_ssm_re, state_ssm_im, state_conv, cache_k, cache_v, page_table,
              a_pre_g, a_post_g, ssm_lam_re, ssm_lam_im, ssm_log_dt, ssm_b_re, ssm_b_im, ssm_c_re, ssm_c_im,
              ssm_d, glu_w, kv_norm_g, w_k, w_v, b_pre_g, b_post_g, w_q, lam_q1, lam_k1, lam_q2, lam_k2,
              sub_g, w_o, f_pre_g, f_post_g, w_up, conv_w, conv_b, w_down):
    p = {'a_pre_g': a_pre_g, 'a_post_g': a_post_g, 'ssm_lam_re': ssm_lam_re, 'ssm_lam_im': ssm_lam_im,
         'ssm_log_dt': ssm_log_dt, 'ssm_b_re': ssm_b_re, 'ssm_b_im': ssm_b_im, 'ssm_c_re': ssm_c_re,
         'ssm_c_im': ssm_c_im, 'ssm_d': ssm_d, 'glu_w': glu_w, 'kv_norm_g': kv_norm_g, 'w_k': w_k, 'w_v': w_v,
         'b_pre_g': b_pre_g, 'b_post_g': b_post_g, 'w_q': w_q, 'lam_q1': lam_q1, 'lam_k1': lam_k1,
         'lam_q2': lam_q2, 'lam_k2': lam_k2, 'sub_g': sub_g, 'w_o': w_o, 'f_pre_g': f_pre_g,
         'f_post_g': f_post_g, 'w_up': w_up, 'conv_w': conv_w, 'conv_b': conv_b, 'w_down': w_down}
    st_dtype = state_ssm_re.dtype

    bsz = x_prompt.shape[0]
    h0_p = jnp.zeros((N_A_LAYERS, bsz, N_GROUPS, SSM_STATE), jnp.complex64)
    conv0_p = jnp.zeros((DEPTH, bsz, CONV_W - 1, 2 * D_FF), x_prompt.dtype)
    y_prompt, ssm_p, conv_prompt, k_prompt, v_prompt = trunk(x_prompt, h0_p, conv0_p, None, None, p)

    dec_b, n_pages = page_table.shape
    past_len = n_pages * cache_k.shape[1]
    past_k = cache_k[page_table].reshape(dec_b, past_len, N_HEADS, 2 * HEAD_DIM)
    past_v = cache_v[page_table].reshape(dec_b, past_len, N_HEADS, V_DIM)
    h0_s = lax.complex(state_ssm_re.astype(jnp.float32), state_ssm_im.astype(jnp.float32))
    y_sample, ssm_s, conv_sample, k_sample, v_sample = trunk(x_sample, h0_s, state_conv, past_k, past_v, p)

    ssm_re_prompt = jnp.real(ssm_p).astype(st_dtype)
    ssm_im_prompt = jnp.imag(ssm_p).astype(st_dtype)
    ssm_re_sample = jnp.real(ssm_s).astype(st_dtype)
    ssm_im_sample = jnp.imag(ssm_s).astype(st_dtype)
    return (y_prompt, y_sample, ssm_re_prompt, ssm_im_prompt, conv_prompt, k_prompt, v_prompt,
            ssm_re_sample, ssm_im_sample, conv_sample, k_sample, v_sample)
```

```python
import functools
import math

import jax
import jax.numpy as jnp
from jax import lax
from jax.experimental import pallas as pl
from jax.experimental.pallas import tpu as pltpu

D_MODEL = 1024
SSM_GROUP = 16
N_GROUPS = D_MODEL // SSM_GROUP
SSM_STATE = 64
N_HEADS = 8
HEAD_DIM = 64
V_DIM = 2 * HEAD_DIM
D_FF = 2816
CONV_W = 3
NORM_EPS = 1e-6
N_A_LAYERS = 1

CHUNK = 16
STATE2 = 2 * SSM_STATE
CHUNK_W = CHUNK * SSM_GROUP
SCAN_STEPS = 7
ACOLS = 8
HIGHEST = lax.Precision.HIGHEST
BF16 = jnp.bfloat16
F32 = jnp.float32
VMEM_LIMIT = 56 * 1024 * 1024


def _cparams(sem, vmem=VMEM_LIMIT):
    return pltpu.CompilerParams(dimension_semantics=sem, vmem_limit_bytes=vmem)


def _resident(shape):
    zeros = (0,) * len(shape)
    return pl.BlockSpec(shape, lambda *_: zeros, pipeline_mode=pl.Buffered(1))


def _inv_rms(x):
    return lax.rsqrt(jnp.mean(x * x, axis=-1, keepdims=True) + NORM_EPS)


def _gelu(x):
    c = math.sqrt(2.0 / math.pi)
    return x * (0.5 * (1.0 + jnp.tanh(c * (x + 0.044715 * (x * x * x)))))


def _dot(a, b):
    return jnp.dot(a, b, preferred_element_type=F32)


def _s5_prep_kernel(lre_c, lim_c, lre_r, lim_r, ldt, bre, bim, cre, cim,
                    wq_ref, zre_ref, zim_ref, kmat_ref, acol_ref):
    P = SSM_STATE
    dt = jnp.exp(ldt[0])

    def cpow(ar, ai, n):
        m = jnp.exp(n * ar)
        return m * jnp.cos(n * ai), m * jnp.sin(n * ai)

    lr, li = lre_c[0], lim_c[0]
    ar, ai = lr * dt, li * dt
    lbr, lbi = cpow(ar, ai, 1.0)
    den = lr * lr + li * li
    nr, ni = lbr - 1.0, lbi
    cr = (nr * lr + ni * li) / den
    ci = (ni * lr - nr * li) / den
    bbr = cr * bre[0] - ci * bim[0]
    bbi = cr * bim[0] + ci * bre[0]
    for s in range(CHUNK):
        pr, pi = cpow(ar, ai, float(CHUNK - 1 - s))
        wq_ref[0, s, 0:P, :] = pr * bbr - pi * bbi
        wq_ref[0, s, P:2 * P, :] = pr * bbi + pi * bbr
    for i in range(ACOLS):
        n = float(CHUNK * 2 ** i) if i < SCAN_STEPS else float(CHUNK // 2)
        pr, pi = cpow(ar, ai, n)
        acol_ref[0, 0:P, i:i + 1] = pr
        acol_ref[0, P:2 * P, i:i + 1] = pi

    arr, air = lre_r[0] * dt, lim_r[0] * dt
    for m in range(CHUNK + 1):
        pr, pi = cpow(arr, air, float(m))
        zre_ref[0, m] = cre[0] * pr - cim[0] * pi
        zim_ref[0, m] = -(cre[0] * pi + cim[0] * pr)
    zr = zre_ref[0, 0:CHUNK].reshape(CHUNK_W, P)
    zi = zim_ref[0, 0:CHUNK].reshape(CHUNK_W, P)
    kmat_ref[0] = (jnp.dot(zr, bbr, precision=HIGHEST, preferred_element_type=F32)
                   + jnp.dot(zi, bbi, precision=HIGHEST, preferred_element_type=F32))


def _s5_prepare(lam_re, lam_im, log_dt, b_re, b_im, c_re, c_im):
    G, P, C = N_GROUPS, SSM_STATE, SSM_GROUP
    col = lambda a: a.reshape(G, P, 1)
    row = lambda a: a.reshape(G, 1, P)
    spec = lambda *blk: pl.BlockSpec((1,) + blk, lambda g: (g,) + (0,) * len(blk))
    wq, zre, zim, kmat, acol = pl.pallas_call(
        _s5_prep_kernel,
        grid=(G,),
        in_specs=[spec(P, 1), spec(P, 1), spec(1, P), spec(1, P), spec(1, 1),
                  spec(P, C), spec(P, C), spec(C, P), spec(C, P)],
        out_specs=[spec(CHUNK, STATE2, C), spec(CHUNK + 1, C, P), spec(CHUNK + 1, C, P),
                   spec(CHUNK_W, C), spec(STATE2, ACOLS)],
        out_shape=[jax.ShapeDtypeStruct((G, CHUNK, STATE2, C), F32),
                   jax.ShapeDtypeStruct((G, CHUNK + 1, C, P), F32),
                   jax.ShapeDtypeStruct((G, CHUNK + 1, C, P), F32),
                   jax.ShapeDtypeStruct((G, CHUNK_W, C), F32),
                   jax.ShapeDtypeStruct((G, STATE2, ACOLS), F32)],
        compiler_params=_cparams(("arbitrary",)),
        name="s5_prep",
    )(col(lam_re), col(lam_im), row(lam_re), row(lam_im), log_dt.reshape(G, 1, 1),
      b_re, b_im, c_re, c_im)
    wt = wq.transpose(0, 2, 1, 3).reshape(G, STATE2, CHUNK_W)
    k4 = kmat.reshape(G, CHUNK, C, C)
    t_idx = jnp.arange(CHUNK)[:, None] - jnp.arange(CHUNK)[None, :]
    toe = jnp.where((t_idx >= 0)[None, :, :, None, None],
                    k4[:, jnp.clip(t_idx, 0, CHUNK - 1)], 0.0)
    toe = toe.transpose(0, 1, 3, 2, 4).reshape(G, CHUNK_W, CHUNK_W)
    m_op = jnp.concatenate([zre[:, 1:], zim[:, 1:]], axis=-1).reshape(G, CHUNK_W, STATE2)
    tm = jnp.concatenate([toe, m_op], axis=-1)
    return wt.astype(BF16), tm.astype(BF16), acol


def _cmul(ar, ai, br, bi):
    return ar * br - ai * bi, ar * bi + ai * br


def _s5_mix_prompt_kernel(x_ref, g_ref, d_ref, wt_ref, tm_ref, acol_ref, z_ref, hout_ref,
                          ut_ref, yt_ref, r_ref, carry_ref, *, nk, gs):
    P, D = SSM_STATE, D_MODEL
    nb, gg = pl.program_id(1), pl.program_id(2)

    @pl.when(gg == 0)
    def _():
        @pl.when(nb == 0)
        def _():
            carry_ref[...] = jnp.zeros_like(carry_ref)
        for s in range(CHUNK):
            xs = x_ref[0, :, s * D:(s + 1) * D]
            r = _inv_rms(xs)
            r_ref[s] = r
            ut_ref[s] = (xs * r * g_ref[...]).T.astype(BF16)

    lane = lax.broadcasted_iota(jnp.int32, (P, nk), 1)
    for gi in range(gs):
        g = gg * gs + gi
        row0 = pl.multiple_of(g * SSM_GROUP, SSM_GROUP)
        ug = ut_ref[:, pl.ds(row0, SSM_GROUP), :].reshape(CHUNK_W, nk)
        acol = acol_ref[gi]
        hin = carry_ref[g]
        s_all = _dot(wt_ref[gi], ug)
        cr, ci = _cmul(acol[:P, 0:1], acol[P:, 0:1], hin[:P], hin[P:])
        sr = s_all[:P] + jnp.where(lane == 0, cr, 0.0)
        si = s_all[P:] + jnp.where(lane == 0, ci, 0.0)
        for i in range(SCAN_STEPS):
            sh = 1 << i
            pr = jnp.where(lane >= sh, pltpu.roll(sr, sh, axis=1), 0.0)
            pi = jnp.where(lane >= sh, pltpu.roll(si, sh, axis=1), 0.0)
            qr, qi = _cmul(acol[:P, i:i + 1], acol[P:, i:i + 1], pr, pi)
            sr, si = sr + qr, si + qi
        hpr = jnp.where(lane >= 1, pltpu.roll(sr, 1, axis=1), hin[:P])
        hpi = jnp.where(lane >= 1, pltpu.roll(si, 1, axis=1), hin[P:])
        hend = jnp.concatenate([sr[:, nk - 1:nk], si[:, nk - 1:nk]], axis=0)
        carry_ref[g] = hend
        hout_ref[0, 0, gi] = hend
        hprev = jnp.concatenate([hpr, hpi], axis=0).astype(BF16)
        y = _dot(tm_ref[gi, :, 0:CHUNK_W], ug) + _dot(tm_ref[gi, :, CHUNK_W:], hprev)
        yt_ref[:, pl.ds(row0, SSM_GROUP), :] = y.reshape(CHUNK, SSM_GROUP, nk)

    @pl.when(gg == pl.num_programs(2) - 1)
    def _():
        for t in range(CHUNK):
            xs = x_ref[0, :, t * D:(t + 1) * D]
            u = xs * r_ref[t] * g_ref[...]
            v = yt_ref[t].T + d_ref[...] * u
            z_ref[0, :, t * D:(t + 1) * D] = _gelu(v).astype(BF16)


def _s5_mix_prompt(x, pre_g, d_skip, wt, tm, acol):
    B, L, D = x.shape
    nk = 1 << SCAN_STEPS
    blk = nk * CHUNK
    assert L % blk == 0
    NB = L // blk
    gs = 8
    G = N_GROUPS
    x2 = x.reshape(B, L // CHUNK, CHUNK * D)
    z2, hout = pl.pallas_call(
        functools.partial(_s5_mix_prompt_kernel, nk=nk, gs=gs),
        grid=(B, NB, G // gs),
        in_specs=[pl.BlockSpec((1, nk, CHUNK * D), lambda b, n, g: (b, n, 0)),
                  pl.BlockSpec((1, D), lambda b, n, g: (0, 0)),
                  pl.BlockSpec((1, D), lambda b, n, g: (0, 0)),
                  pl.BlockSpec((gs, STATE2, CHUNK_W), lambda b, n, g: (g, 0, 0)),
                  pl.BlockSpec((gs, CHUNK_W, CHUNK_W + STATE2), lambda b, n, g: (g, 0, 0)),
                  pl.BlockSpec((gs, STATE2, ACOLS), lambda b, n, g: (g, 0, 0))],
        out_specs=[pl.BlockSpec((1, nk, CHUNK * D), lambda b, n, g: (b, n, 0)),
                   pl.BlockSpec((1, 1, gs, STATE2, 1), lambda b, n, g: (b, n, g, 0, 0))],
        out_shape=[jax.ShapeDtypeStruct((B, L // CHUNK, CHUNK * D), BF16),
                   jax.ShapeDtypeStruct((B, NB, G, STATE2, 1), F32)],
        scratch_shapes=[pltpu.VMEM((CHUNK, D, nk), BF16),
                        pltpu.VMEM((CHUNK, D, nk), F32),
                        pltpu.VMEM((CHUNK, nk, 1), F32),
                        pltpu.VMEM((G, STATE2, 1), F32)],
        compiler_params=_cparams(("arbitrary", "arbitrary", "arbitrary")),
        name="s5_mix_prompt",
    )(x2, pre_g.reshape(1, D), d_skip.reshape(1, D), wt, tm, acol)
    return z2, hout[:, NB - 1]


def _s5_mix_sample_kernel(x_ref, g_ref, d_ref, h0_ref, wt_ref, tm_ref, acol_ref, z_ref, hout_ref,
                          ut_ref, yt_ref, r_ref, *, nseq, ntok, gs):
    P, D = SSM_STATE, D_MODEL
    half = ntok * SSM_GROUP
    gg = pl.program_id(0)

    @pl.when(gg == 0)
    def _():
        for s in range(ntok):
            xs = x_ref[:, s * D:(s + 1) * D]
            r = _inv_rms(xs)
            r_ref[s] = r
            ut_ref[s] = (xs * r * g_ref[...]).T.astype(BF16)

    for gi in range(gs):
        g = gg * gs + gi
        row0 = pl.multiple_of(g * SSM_GROUP, SSM_GROUP)
        ug = ut_ref[:, pl.ds(row0, SSM_GROUP), :].reshape(half, nseq)
        h0 = h0_ref[gi]
        acol = acol_ref[gi]
        s_all = _dot(wt_ref[gi, :, half:], ug)
        er, ei = _cmul(acol[:P, ACOLS - 1:ACOLS], acol[P:, ACOLS - 1:ACOLS], h0[:P], h0[P:])
        hout_ref[gi] = jnp.concatenate([er + s_all[:P], ei + s_all[P:]], axis=0)
        y = _dot(tm_ref[gi, 0:half, 0:half], ug) + _dot(tm_ref[gi, 0:half, CHUNK_W:], h0.astype(BF16))
        yt_ref[:, pl.ds(row0, SSM_GROUP), :] = y.reshape(ntok, SSM_GROUP, nseq)

    @pl.when(gg == pl.num_programs(0) - 1)
    def _():
        for t in range(ntok):
            xs = x_ref[:, t * D:(t + 1) * D]
            u = xs * r_ref[t] * g_ref[...]
            v = yt_ref[t].T + d_ref[...] * u
            z_ref[:, t * D:(t + 1) * D] = _gelu(v).astype(BF16)


def _s5_mix_sample(x, h0, pre_g, d_skip, wt, tm, acol):
    S, T, D = x.shape
    assert T * 2 == CHUNK
    gs = 8
    G = N_GROUPS
    z2, hout = pl.pallas_call(
        functools.partial(_s5_mix_sample_kernel, nseq=S, ntok=T, gs=gs),
        grid=(G // gs,),
        in_specs=[pl.BlockSpec((S, T * D), lambda g: (0, 0)),
                  pl.BlockSpec((1, D), lambda g: (0, 0)),
                  pl.BlockSpec((1, D), lambda g: (0, 0)),
                  pl.BlockSpec((gs, STATE2, S), lambda g: (g, 0, 0)),
                  pl.BlockSpec((gs, STATE2, CHUNK_W), lambda g: (g, 0, 0)),
                  pl.BlockSpec((gs, CHUNK_W, CHUNK_W + STATE2), lambda g: (g, 0, 0)),
                  pl.BlockSpec((gs, STATE2, ACOLS), lambda g: (g, 0, 0))],
        out_specs=[pl.BlockSpec((S, T * D), lambda g: (0, 0)),
                   pl.BlockSpec((gs, STATE2, S), lambda g: (g, 0, 0))],
        out_shape=[jax.ShapeDtypeStruct((S, T * D), BF16),
                   jax.ShapeDtypeStruct((G, STATE2, S), F32)],
        scratch_shapes=[pltpu.VMEM((T, D, S), BF16),
                        pltpu.VMEM((T, D, S), F32),
                        pltpu.VMEM((T, S, 1), F32)],
        compiler_params=_cparams(("arbitrary",)),
        name="s5_mix_sample",
    )(x.reshape(S, T * D), pre_g.reshape(1, D), d_skip.reshape(1, D), h0, wt, tm, acol)
    return z2, hout


def _tail_kernel(a_ref, w_ref, g_ref, x_ref, o_ref, *, glu):
    y = _dot(a_ref[...], w_ref[...])
    if glu:
        n = y.shape[-1] // 2
        y = y[:, :n] * jax.nn.sigmoid(y[:, n:])
    o_ref[...] = x_ref[...] + y * _inv_rms(y) * g_ref[...]


def _tail(a2, w, g, x2, *, glu, tm):
    R = a2.shape[0]
    K = w.shape[0]
    D = D_MODEL
    n = a2.shape[1] // K
    tm = min(tm, R)
    assert R % tm == 0 and x2.shape == (R, n * D)
    return pl.pallas_call(
        functools.partial(_tail_kernel, glu=glu),
        grid=(R // tm, n),
        in_specs=[pl.BlockSpec((tm, K), lambda i, s: (i, s)),
                  _resident(w.shape),
                  pl.BlockSpec((1, D), lambda i, s: (0, 0)),
                  pl.BlockSpec((tm, D), lambda i, s: (i, s))],
        out_specs=pl.BlockSpec((tm, D), lambda i, s: (i, s)),
        out_shape=jax.ShapeDtypeStruct((R, n * D), F32),
        compiler_params=_cparams(("parallel", "parallel")),
        name="glu_tail" if glu else "oproj_tail",
    )(a2, w, g.reshape(1, D), x2)


def _kvq_kernel(x_ref, gkv_ref, gq_ref, wk_ref, wv_ref, wq_ref,
                k_ref, v_ref, kb_ref, vb_ref, qb_ref):
    x = x_ref[...]
    xr = x * _inv_rms(x)
    kv_in = (xr * gkv_ref[...]).astype(BF16)
    xn = (xr * gq_ref[...]).astype(BF16)
    k = _dot(kv_in, wk_ref[...])
    v = _dot(kv_in, wv_ref[...])
    q = _dot(xn, wq_ref[...]) * (HEAD_DIM ** -0.5)
    k_ref[...] = k
    v_ref[...] = v
    kb_ref[...] = k.astype(BF16)
    vb_ref[...] = v.astype(BF16)
    qb_ref[...] = q.astype(qb_ref.dtype)


def _kvq(x2, g_kv, g_q, wk, wv, wq, *, tm, q_dtype):
    R, D = x2.shape
    tm = min(tm, R)
    assert R % tm == 0
    tile = pl.BlockSpec((tm, D), lambda i: (i, 0))
    vec = pl.BlockSpec((1, D), lambda i: (0, 0))
    return pl.pallas_call(
        _kvq_kernel,
        grid=(R // tm,),
        in_specs=[tile, vec, vec, _resident(wk.shape), _resident(wv.shape), _resident(wq.shape)],
        out_specs=[tile] * 5,
        out_shape=[jax.ShapeDtypeStruct((R, D), F32), jax.ShapeDtypeStruct((R, D), F32),
                   jax.ShapeDtypeStruct((R, D), BF16), jax.ShapeDtypeStruct((R, D), BF16),
                   jax.ShapeDtypeStruct((R, D), q_dtype)],
        compiler_params=_cparams(("parallel",)),
        name="kvq_proj",
    )(x2, g_kv.reshape(1, D), g_q.reshape(1, D), wk, wv, wq)


FF_CHUNK = D_FF // 2


def _ffn_cols(c):
    return (slice(c * FF_CHUNK, (c + 1) * FF_CHUNK),
            slice(D_FF + c * FF_CHUNK, D_FF + (c + 1) * FF_CHUNK))


def _ffn_prompt_kernel(x_ref, gpre_ref, gpost_ref, wup_ref, cw_ref, cb_ref, wdn_ref,
                       o_ref, cs_ref, carry_ref, *, tm):
    @pl.when(pl.program_id(1) == 0)
    def _():
        carry_ref[...] = jnp.zeros_like(carry_ref)

    x = x_ref[0]
    xn = (x * _inv_rms(x) * gpre_ref[...]).astype(BF16)
    rows = lax.broadcasted_iota(jnp.int32, (tm, 1), 0)
    f = jnp.zeros((tm, D_MODEL), F32)
    for c in range(D_FF // FF_CHUNK):
        halves = []
        for cols in _ffn_cols(c):
            h = _dot(xn, wup_ref[:, cols])
            c0, c1 = carry_ref[6:7, cols], carry_ref[7:8, cols]
            h1 = jnp.where(rows == 0, c1, pltpu.roll(h, 1, axis=0))
            h2 = jnp.where(rows == 0, c0, jnp.where(rows == 1, c1, pltpu.roll(h, 2, axis=0)))
            halves.append(cb_ref[:, cols] + cw_ref[2:3, cols] * h
                          + cw_ref[1:2, cols] * h1 + cw_ref[0:1, cols] * h2)
            carry_ref[:, cols] = h[tm - 8:tm]
            cs_ref[0, :, cols] = h[tm - 2:tm]
        a = (_gelu(halves[0]) * halves[1]).astype(BF16)
        f = f + _dot(a, wdn_ref[c * FF_CHUNK:(c + 1) * FF_CHUNK, :])
    o_ref[0] = x + f * _inv_rms(f) * gpost_ref[...]


def _ffn_prompt(x, gpre, gpost, wup, cw, cb, wdn, *, tm):
    B, L, D = x.shape
    tm = min(tm, L)
    assert L % tm == 0
    F2 = 2 * D_FF
    vec = pl.BlockSpec((1, D), lambda b, t: (0, 0))
    return pl.pallas_call(
        functools.partial(_ffn_prompt_kernel, tm=tm),
        grid=(B, L // tm),
        in_specs=[pl.BlockSpec((1, tm, D), lambda b, t: (b, t, 0)), vec, vec,
                  _resident(wup.shape), _resident((CONV_W, F2)), _resident((1, F2)),
                  _resident(wdn.shape)],
        out_specs=[pl.BlockSpec((1, tm, D), lambda b, t: (b, t, 0)),
                   pl.BlockSpec((1, CONV_W - 1, F2), lambda b, t: (b, 0, 0))],
        out_shape=[jax.ShapeDtypeStruct((B, L, D), F32),
                   jax.ShapeDtypeStruct((B, CONV_W - 1, F2), F32)],
        scratch_shapes=[pltpu.VMEM((8, F2), F32)],
        compiler_params=_cparams(("arbitrary", "arbitrary")),
        name="ffn_prompt",
    )(x, gpre.reshape(1, D), gpost.reshape(1, D), wup, cw, cb.reshape(1, F2), wdn)


def _ffn_sample_kernel(x_ref, buf_ref, gpre_ref, gpost_ref, wup_ref, cw_ref, cb_ref, wdn_ref,
                       o_ref, cs_ref, *, tm, ntok):
    nsq = tm // ntok
    x = x_ref[...]
    xn = (x * _inv_rms(x) * gpre_ref[...]).astype(BF16)
    sub = lax.broadcasted_iota(jnp.int32, (tm, 1), 0) % ntok
    f = jnp.zeros((tm, D_MODEL), F32)
    for c in range(D_FF // FF_CHUNK):
        halves = []
        for cols in _ffn_cols(c):
            h = _dot(xn, wup_ref[:, cols])
            spread = lambda b: jnp.broadcast_to(b, (nsq, ntok, FF_CHUNK)).reshape(tm, FF_CHUNK)
            b0, b1 = spread(buf_ref[:, 0:1, cols]), spread(buf_ref[:, 1:2, cols])
            h1 = jnp.where(sub == 0, b1, pltpu.roll(h, 1, axis=0))
            h2 = jnp.where(sub == 0, b0, jnp.where(sub == 1, b1, pltpu.roll(h, 2, axis=0)))
            halves.append(cb_ref[:, cols] + cw_ref[2:3, cols] * h
                          + cw_ref[1:2, cols] * h1 + cw_ref[0:1, cols] * h2)
            cs_ref[:, :, cols] = h.reshape(nsq, ntok, FF_CHUNK)[:, ntok - 2:ntok, :]
        a = (_gelu(halves[0]) * halves[1]).astype(BF16)
        f = f + _dot(a, wdn_ref[c * FF_CHUNK:(c + 1) * FF_CHUNK, :])
    o_ref[...] = x + f * _inv_rms(f) * gpost_ref[...]


def _ffn_sample(x, buf, gpre, gpost, wup, cw, cb, wdn, *, tm):
    S, T, D = x.shape
    assert T == 8 and tm % T == 0 and (S * T) % tm == 0
    F2 = 2 * D_FF
    nsq = tm // T
    vec = pl.BlockSpec((1, D), lambda i: (0, 0))
    out, cs = pl.pallas_call(
        functools.partial(_ffn_sample_kernel, tm=tm, ntok=T),
        grid=(S * T // tm,),
        in_specs=[pl.BlockSpec((tm, D), lambda i: (i, 0)),
                  pl.BlockSpec((nsq, CONV_W - 1, F2), lambda i: (i, 0, 0)), vec, vec,
                  _resident(wup.shape), _resident((CONV_W, F2)), _resident((1, F2)),
                  _resident(wdn.shape)],
        out_specs=[pl.BlockSpec((tm, D), lambda i: (i, 0)),
                   pl.BlockSpec((nsq, CONV_W - 1, F2), lambda i: (i, 0, 0))],
        out_shape=[jax.ShapeDtypeStruct((S * T, D), F32),
                   jax.ShapeDtypeStruct((S, CONV_W - 1, F2), F32)],
        compiler_params=_cparams(("parallel",)),
        name="ffn_sample",
    )(x.reshape(S * T, D), buf, gpre.reshape(1, D), gpost.reshape(1, D), wup, cw,
      cb.reshape(1, F2), wdn)
    return out.reshape(S, T, D), cs


def _lam_value(lq1, lk1, lq2, lk2, lam_init):
    return (jnp.exp(jnp.sum(lq1[...] * lk1[...], axis=-1, keepdims=True))
            - jnp.exp(jnp.sum(lq2[...] * lk2[...], axis=-1, keepdims=True)) + lam_init)


def _sub_norm(o, subg, lam_init):
    return o * _inv_rms(o) * subg * (1.0 - lam_init)


def _flash_kernel(qi_ref, ki_ref, q_ref, k_ref, v_ref, lq1, lk1, lq2, lk2, subg_ref, o_ref,
                  m_ref, l_ref, acc_ref, *, tq, lam_init):
    h, step = pl.program_id(1), pl.program_id(2)
    qi, ki = qi_ref[step], ki_ref[step]

    @pl.when(ki == 0)
    def _():
        m_ref[...] = jnp.full_like(m_ref, -jnp.inf)
        l_ref[...] = jnp.zeros_like(l_ref)
        acc_ref[...] = jnp.zeros_like(acc_ref)

    def absorb(diagonal):
        q, k, v = q_ref[...], k_ref[...], v_ref[...]
        col = lax.broadcasted_iota(jnp.int32, (1, V_DIM), 1)
        kpos = lax.broadcasted_iota(jnp.int32, (1, tq), 1) + (ki - qi) * tq
        slope = jnp.exp2(-jnp.full((1, tq), h + 1, jnp.int32).astype(F32))
        bias = slope * kpos.astype(F32)
        if diagonal:
            allowed = kpos <= lax.broadcasted_iota(jnp.int32, (tq, 1), 0)
        for j in range(2):
            qj = jnp.where((col >= HEAD_DIM) == (j == 1), q, jnp.zeros_like(q))
            s = lax.dot_general(qj, k, (((1,), (1,)), ((), ())), preferred_element_type=F32) + bias
            if diagonal:
                s = jnp.where(allowed, s, -jnp.inf)
            m_old = m_ref[j]
            m_new = jnp.maximum(m_old, jnp.max(s, axis=-1, keepdims=True))
            alpha = jnp.exp(m_old - m_new)
            p = jnp.exp(s - m_new)
            l_ref[j] = alpha * l_ref[j] + jnp.sum(p, axis=-1, keepdims=True)
            acc_ref[j] = alpha * acc_ref[j] + _dot(p.astype(BF16), v)
            m_ref[j] = m_new

    @pl.when(ki < qi)
    def _():
        absorb(False)

    @pl.when(ki == qi)
    def _():
        absorb(True)
        lam = _lam_value(lq1, lk1, lq2, lk2, lam_init)
        o = acc_ref[0] / l_ref[0] - lam * (acc_ref[1] / l_ref[1])
        o_ref[...] = _sub_norm(o, subg_ref[...], lam_init).astype(o_ref.dtype)


def _flash_prompt(qb, kb, vb, lq1, lk1, lq2, lk2, subg, *, B, L, tq, lam_init):
    tq = min(tq, L)
    assert L % tq == 0
    nq = L // tq
    pairs = [(i, j) for i in range(nq) for j in range(i + 1)]
    qi_tab = jnp.asarray([p[0] for p in pairs], jnp.int32)
    ki_tab = jnp.asarray([p[1] for p in pairs], jnp.int32)
    lvec = pl.BlockSpec((1, HEAD_DIM), lambda b, h, s, qt, kt: (0, 0))
    grid_spec = pltpu.PrefetchScalarGridSpec(
        num_scalar_prefetch=2,
        grid=(B, N_HEADS, len(pairs)),
        in_specs=[pl.BlockSpec((tq, V_DIM), lambda b, h, s, qt, kt: (b * nq + qt[s], h)),
                  pl.BlockSpec((tq, V_DIM), lambda b, h, s, qt, kt: (b * nq + kt[s], h)),
                  pl.BlockSpec((tq, V_DIM), lambda b, h, s, qt, kt: (b * nq + kt[s], h)),
                  lvec, lvec, lvec, lvec,
                  pl.BlockSpec((1, V_DIM), lambda b, h, s, qt, kt: (0, 0))],
        out_specs=pl.BlockSpec((tq, V_DIM), lambda b, h, s, qt, kt: (b * nq + qt[s], h)),
        scratch_shapes=[pltpu.VMEM((2, tq, 1), F32), pltpu.VMEM((2, tq, 1), F32),
                        pltpu.VMEM((2, tq, V_DIM), F32)])
    return pl.pallas_call(
        functools.partial(_flash_kernel, tq=tq, lam_init=lam_init),
        grid_spec=grid_spec,
        out_shape=jax.ShapeDtypeStruct((B * L, N_HEADS * V_DIM), BF16),
        compiler_params=_cparams(("parallel", "parallel", "arbitrary")),
        name="flash_prompt",
    )(qi_tab, ki_tab, qb, kb, vb, lq1, lk1, lq2, lk2, subg)


def _paged_kernel(pt_ref, q_ref, kc_ref, vc_ref, kn_ref, vn_ref, lq1, lk1, lq2, lk2, subg_ref,
                  o_ref, q8_ref, m_ref, l_ref, acc_ref, *, ntok, page, past_len, lam_init):
    pg = pl.program_id(1)
    n_pg = pl.num_programs(1)
    W = N_HEADS * 2 * ntok
    HW = N_HEADS * V_DIM

    colw = lax.broadcasted_iota(jnp.int32, (1, W), 1)
    slope_c = jnp.exp2(-(colw // (2 * ntok) + 1).astype(F32))
    qpos_c = past_len + colw % ntok

    @pl.when(pg == 0)
    def _():
        m_ref[...] = jnp.full_like(m_ref, -jnp.inf)
        l_ref[...] = jnp.zeros_like(l_ref)
        acc_ref[...] = jnp.zeros_like(acc_ref)
        qt = jnp.concatenate([q_ref[...]] * (W // ntok), axis=0)
        r = lax.broadcasted_iota(jnp.int32, (W, 1), 0)
        c = lax.broadcasted_iota(jnp.int32, (1, HW), 1)
        keep = (c // HEAD_DIM) == (r // ntok)
        q8_ref[...] = jnp.where(keep, qt, 0.0).astype(BF16)

    def col_of(rowvec):
        return jnp.broadcast_to(rowvec, (W, W)).T[:, 0:1]

    def absorb(k, v, kpos, allowed):
        s = lax.dot_general(k, q8_ref[...], (((1,), (1,)), ((), ())), preferred_element_type=F32)
        s = s - slope_c * (qpos_c - kpos).astype(F32)
        if allowed is not None:
            s = jnp.where(allowed, s, -jnp.inf)
        m_old = m_ref[...]
        m_new = jnp.maximum(m_old, jnp.max(s, axis=0, keepdims=True))
        alpha = jnp.exp(m_old - m_new)
        p = jnp.exp(s - m_new)
        l_ref[...] = alpha * l_ref[...] + jnp.sum(p, axis=0, keepdims=True)
        m_ref[...] = m_new
        pv = lax.dot_general(p.astype(BF16), v, (((0,), (0,)), ((), ())), preferred_element_type=F32)
        alpha_col = col_of(alpha)
        for hh in range(N_HEADS):
            rs = slice(hh * 2 * ntok, (hh + 1) * 2 * ntok)
            cs = slice(hh * V_DIM, (hh + 1) * V_DIM)
            acc_ref[rs, cs] = alpha_col[rs] * acc_ref[rs, cs] + pv[rs, cs]

    kpos_pg = lax.broadcasted_iota(jnp.int32, (page, 1), 0) + pg * page
    absorb(kc_ref[0].astype(BF16), vc_ref[0].astype(BF16), kpos_pg, None)

    @pl.when(pg == n_pg - 1)
    def _():
        pad = jnp.zeros((page - ntok, HW), F32)
        row = lax.broadcasted_iota(jnp.int32, (page, 1), 0)
        kpos_n = row + past_len
        allowed = (row < ntok) & (kpos_n <= qpos_c)
        absorb(jnp.concatenate([kn_ref[...], pad], axis=0).astype(BF16),
               jnp.concatenate([vn_ref[...], pad], axis=0).astype(BF16), kpos_n, allowed)
        lam = _lam_value(lq1, lk1, lq2, lk2, lam_init)
        l_col = col_of(l_ref[...])
        for hh in range(N_HEADS):
            r0 = hh * 2 * ntok
            cs = slice(hh * V_DIM, (hh + 1) * V_DIM)
            o0 = acc_ref[r0:r0 + ntok, cs] / l_col[r0:r0 + ntok]
            o1 = acc_ref[r0 + ntok:r0 + 2 * ntok, cs] / l_col[r0 + ntok:r0 + 2 * ntok]
            o_ref[:, cs] = _sub_norm(o0 - lam * o1, subg_ref[...], lam_init)


def _paged_sample(page_table, q, cache_k, cache_v, k_new, v_new, lq1, lk1, lq2, lk2, subg,
                  *, ntok, lam_init):
    S, n_pages = page_table.shape
    page = cache_k.shape[1]
    HW = N_HEADS * V_DIM
    W = N_HEADS * 2 * ntok
    assert W == page
    tok = pl.BlockSpec((ntok, HW), lambda s, p, pt: (s, 0))
    pgs = pl.BlockSpec((1, page, HW), lambda s, p, pt: (pt[s, p], 0, 0))
    lvec = pl.BlockSpec((1, HEAD_DIM), lambda s, p, pt: (0, 0))
    grid_spec = pltpu.PrefetchScalarGridSpec(
        num_scalar_prefetch=1,
        grid=(S, n_pages),
        in_specs=[tok, pgs, pgs, tok, tok, lvec, lvec, lvec, lvec,
                  pl.BlockSpec((1, V_DIM), lambda s, p, pt: (0, 0))],
        out_specs=tok,
        scratch_shapes=[pltpu.VMEM((W, HW), BF16), pltpu.VMEM((1, W), F32), pltpu.VMEM((1, W), F32),
                        pltpu.VMEM((W, HW), F32)])
    return pl.pallas_call(
        functools.partial(_paged_kernel, ntok=ntok, page=page, past_len=n_pages * page,
                          lam_init=lam_init),
        grid_spec=grid_spec,
        out_shape=jax.ShapeDtypeStruct((S * ntok, HW), F32),
        compiler_params=_cparams(("arbitrary", "arbitrary")),
        name="paged_sample",
    )(page_table, q, cache_k, cache_v, k_new, v_new, lq1, lk1, lq2, lk2, subg)


def kernel(x_prompt, x_sample, state_ssm_re, state_ssm_im, state_conv, cache_k, cache_v, page_table,
           a_pre_g, a_post_g, ssm_lam_re, ssm_lam_im, ssm_log_dt, ssm_b_re, ssm_b_im, ssm_c_re,
           ssm_c_im, ssm_d, glu_w, kv_norm_g, w_k, w_v, b_pre_g, b_post_g, w_q, lam_q1, lam_k1,
           lam_q2, lam_k2, sub_g, w_o, f_pre_g, f_post_g, w_up, conv_w, conv_b, w_down):
    B, L, D = x_prompt.shape
    S, T, _ = x_sample.shape
    P, G = SSM_STATE, N_GROUPS
    HW = N_HEADS * V_DIM
    st_dtype = state_ssm_re.dtype

    wt, tm_op, acol = _s5_prepare(ssm_lam_re[0], ssm_lam_im[0], ssm_log_dt[0], ssm_b_re[0],
                                  ssm_b_im[0], ssm_c_re[0], ssm_c_im[0])
    glu_b = glu_w[0].astype(BF16)
    zp, hp = _s5_mix_prompt(x_prompt, a_pre_g[0], ssm_d[0], wt, tm_op, acol)
    xp = _tail(zp.reshape(B * L // CHUNK, CHUNK * D), glu_b, a_post_g[0],
               x_prompt.reshape(B * L // CHUNK, CHUNK * D), glu=True, tm=512).reshape(B, L, D)
    h0 = jnp.concatenate([state_ssm_re[0].astype(F32), state_ssm_im[0].astype(F32)], axis=-1)
    zs, hs = _s5_mix_sample(x_sample, h0.transpose(1, 2, 0), a_pre_g[0], ssm_d[0], wt, tm_op, acol)
    xs = _tail(zs, glu_b, a_post_g[0], x_sample.reshape(S, T * D), glu=True, tm=512).reshape(S, T, D)

    hp = hp.reshape(B, G, 2 * P)
    hs = hs.transpose(2, 0, 1)
    ssm_re_p, ssm_im_p = hp[None, ..., :P].astype(st_dtype), hp[None, ..., P:].astype(st_dtype)
    ssm_re_s, ssm_im_s = hs[None, ..., :P].astype(st_dtype), hs[None, ..., P:].astype(st_dtype)

    wup_b, wdn_b = w_up.astype(BF16), w_down.astype(BF16)
    xp, conv_p0 = _ffn_prompt(xp, f_pre_g[0], f_post_g[0], wup_b[0], conv_w[0], conv_b[0], wdn_b[0], tm=256)
    xs, conv_s0 = _ffn_sample(xs, state_conv[0], f_pre_g[0], f_post_g[0], wup_b[0], conv_w[0],
                              conv_b[0], wdn_b[0], tm=256)

    lam_init = 0.8 - 0.6 * math.exp(-0.3 * N_A_LAYERS)
    wk_b, wv_b, wq_b, wo_b = w_k.astype(BF16), w_v.astype(BF16), w_q[0].astype(BF16), w_o[0].astype(BF16)
    lvec = lambda a: a[0].reshape(1, HEAD_DIM).astype(F32)
    lams = (lvec(lam_q1), lvec(lam_k1), lvec(lam_q2), lvec(lam_k2))
    subg = sub_g[0].reshape(1, V_DIM)

    xp2 = xp.reshape(B * L, D)
    kp, vp, kpb, vpb, qpb = _kvq(xp2, kv_norm_g, b_pre_g[0], wk_b, wv_b, wq_b, tm=512, q_dtype=BF16)
    op = _flash_prompt(qpb, kpb, vpb, *lams, subg, B=B, L=L, tq=512, lam_init=lam_init)
    xp = _tail(op, wo_b, b_post_g[0], xp2, glu=False, tm=512).reshape(B, L, D)

    xs2 = xs.reshape(S * T, D)
    ks, vs, _, _, qs = _kvq(xs2, kv_norm_g, b_pre_g[0], wk_b, wv_b, wq_b, tm=512, q_dtype=F32)
    n_pool, page = cache_k.shape[:2]
    os_ = _paged_sample(page_table, qs, cache_k.reshape(n_pool, page, HW), cache_v.reshape(n_pool, page, HW),
                        ks, vs, *lams, subg, ntok=T, lam_init=lam_init)
    xs = _tail(os_.astype(BF16), wo_b, b_post_g[0], xs2, glu=False, tm=512).reshape(S, T, D)

    xp, conv_p1 = _ffn_prompt(xp, f_pre_g[1], f_post_g[1], wup_b[1], conv_w[1], conv_b[1], wdn_b[1], tm=256)
    xs, conv_s1 = _ffn_sample(xs, state_conv[1], f_pre_g[1], f_post_g[1], wup_b[1], conv_w[1],
                              conv_b[1], wdn_b[1], tm=256)

    return (xp, xs, ssm_re_p, ssm_im_p, jnp.stack([conv_p0, conv_p1]),
            kp.reshape(B, L, N_HEADS, 2 * HEAD_DIM), vp.reshape(B, L, N_HEADS, V_DIM),
            ssm_re_s, ssm_im_s, jnp.stack([conv_s0, conv_s1]),
            ks.reshape(S, T, N_HEADS, 2 * HEAD_DIM), vs.reshape(S, T, N_HEADS, V_DIM))
```

```python
import functools
import math

import jax
import jax.numpy as jnp
from jax import lax
from jax.experimental import pallas as pl
from jax.experimental.pallas import tpu as pltpu

D_MODEL = 1024
SSM_GROUP = 16
N_GROUPS = D_MODEL // SSM_GROUP
SSM_STATE = 64
N_HEADS = 8
HEAD_DIM = 64
V_DIM = 2 * HEAD_DIM
D_FF = 2816
CONV_W = 3
NORM_EPS = 1e-6
N_A_LAYERS = 1

CHUNK = 16
STATE2 = 2 * SSM_STATE
CHUNK_W = CHUNK * SSM_GROUP
SCAN_STEPS = 7
ACOLS = 8
HIGHEST = lax.Precision.HIGHEST
BF16 = jnp.bfloat16
F32 = jnp.float32
VMEM_LIMIT = 56 * 1024 * 1024


def _cparams(sem, vmem=VMEM_LIMIT):
    return pltpu.CompilerParams(dimension_semantics=sem, vmem_limit_bytes=vmem)


def _resident(shape):
    zeros = (0,) * len(shape)
    return pl.BlockSpec(shape, lambda *_: zeros, pipeline_mode=pl.Buffered(1))


def _inv_rms(x):
    return lax.rsqrt(jnp.mean(x * x, axis=-1, keepdims=True) + NORM_EPS)


def _gelu(x):
    c = math.sqrt(2.0 / math.pi)
    return x * (0.5 * (1.0 + jnp.tanh(c * (x + 0.044715 * (x * x * x)))))


def _dot(a, b):
    return jnp.dot(a, b, preferred_element_type=F32)


def _s5_prep_kernel(lre_c, lim_c, lre_r, lim_r, ldt, bre, bim, cre, cim,
                    wq_ref, zre_ref, zim_ref, kmat_ref, acol_ref):
    P = SSM_STATE
    dt = jnp.exp(ldt[0])

    def cpow(ar, ai, n):
        m = jnp.exp(n * ar)
        return m * jnp.cos(n * ai), m * jnp.sin(n * ai)

    lr, li = lre_c[0], lim_c[0]
    ar, ai = lr * dt, li * dt
    lbr, lbi = cpow(ar, ai, 1.0)
    den = lr * lr + li * li
    nr, ni = lbr - 1.0, lbi
    cr = (nr * lr + ni * li) / den
    ci = (ni * lr - nr * li) / den
    bbr = cr * bre[0] - ci * bim[0]
    bbi = cr * bim[0] + ci * bre[0]
    for s in range(CHUNK):
        pr, pi = cpow(ar, ai, float(CHUNK - 1 - s))
        wq_ref[0, s, 0:P, :] = pr * bbr - pi * bbi
        wq_ref[0, s, P:2 * P, :] = pr * bbi + pi * bbr
    for i in range(ACOLS):
        n = float(CHUNK * 2 ** i) if i < SCAN_STEPS else float(CHUNK // 2)
        pr, pi = cpow(ar, ai, n)
        acol_ref[0, 0:P, i:i + 1] = pr
        acol_ref[0, P:2 * P, i:i + 1] = pi

    arr, air = lre_r[0] * dt, lim_r[0] * dt
    for m in range(CHUNK + 1):
        pr, pi = cpow(arr, air, float(m))
        zre_ref[0, m] = cre[0] * pr - cim[0] * pi
        zim_ref[0, m] = -(cre[0] * pi + cim[0] * pr)
    zr = zre_ref[0, 0:CHUNK].reshape(CHUNK_W, P)
    zi = zim_ref[0, 0:CHUNK].reshape(CHUNK_W, P)
    kmat_ref[0] = (jnp.dot(zr, bbr, precision=HIGHEST, preferred_element_type=F32)
                   + jnp.dot(zi, bbi, precision=HIGHEST, preferred_element_type=F32))


def _s5_prepare(lam_re, lam_im, log_dt, b_re, b_im, c_re, c_im):
    G, P, C = N_GROUPS, SSM_STATE, SSM_GROUP
    col = lambda a: a.reshape(G, P, 1)
    row = lambda a: a.reshape(G, 1, P)
    spec = lambda *blk: pl.BlockSpec((1,) + blk, lambda g: (g,) + (0,) * len(blk))
    wq, zre, zim, kmat, acol = pl.pallas_call(
        _s5_prep_kernel,
        grid=(G,),
        in_specs=[spec(P, 1), spec(P, 1), spec(1, P), spec(1, P), spec(1, 1),
                  spec(P, C), spec(P, C), spec(C, P), spec(C, P)],
        out_specs=[spec(CHUNK, STATE2, C), spec(CHUNK + 1, C, P), spec(CHUNK + 1, C, P),
                   spec(CHUNK_W, C), spec(STATE2, ACOLS)],
        out_shape=[jax.ShapeDtypeStruct((G, CHUNK, STATE2, C), F32),
                   jax.ShapeDtypeStruct((G, CHUNK + 1, C, P), F32),
                   jax.ShapeDtypeStruct((G, CHUNK + 1, C, P), F32),
                   jax.ShapeDtypeStruct((G, CHUNK_W, C), F32),
                   jax.ShapeDtypeStruct((G, STATE2, ACOLS), F32)],
        compiler_params=_cparams(("arbitrary",)),
        name="s5_prep",
    )(col(lam_re), col(lam_im), row(lam_re), row(lam_im), log_dt.reshape(G, 1, 1),
      b_re, b_im, c_re, c_im)
    wt = wq.transpose(0, 2, 1, 3).reshape(G, STATE2, CHUNK_W)
    k4 = kmat.reshape(G, CHUNK, C, C)
    t_idx = jnp.arange(CHUNK)[:, None] - jnp.arange(CHUNK)[None, :]
    toe = jnp.where((t_idx >= 0)[None, :, :, None, None],
                    k4[:, jnp.clip(t_idx, 0, CHUNK - 1)], 0.0)
    toe = toe.transpose(0, 1, 3, 2, 4).reshape(G, CHUNK_W, CHUNK_W)
    m_op = jnp.concatenate([zre[:, 1:], zim[:, 1:]], axis=-1).reshape(G, CHUNK_W, STATE2)
    tm = jnp.concatenate([toe, m_op], axis=-1)
    return wt.astype(BF16), tm.astype(BF16), acol


def _cmul(ar, ai, br, bi):
    return ar * br - ai * bi, ar * bi + ai * br


def _s5_mix_prompt_kernel(x_ref, g_ref, d_ref, wt_ref, tm_ref, acol_ref, z_ref, hout_ref,
                          ut_ref, yt_ref, r_ref, carry_ref, *, nk, gs):
    P, D = SSM_STATE, D_MODEL
    nb, gg = pl.program_id(1), pl.program_id(2)

    @pl.when(gg == 0)
    def _():
        @pl.when(nb == 0)
        def _():
            carry_ref[...] = jnp.zeros_like(carry_ref)
        for s in range(CHUNK):
            xs = x_ref[0, :, s * D:(s + 1) * D]
            r = _inv_rms(xs)
            r_ref[s] = r
            ut_ref[s] = (xs * r * g_ref[...]).T.astype(BF16)

    lane = lax.broadcasted_iota(jnp.int32, (P, nk), 1)
    for gi in range(gs):
        g = gg * gs + gi
        row0 = pl.multiple_of(g * SSM_GROUP, SSM_GROUP)
        ug = ut_ref[:, pl.ds(row0, SSM_GROUP), :].reshape(CHUNK_W, nk)
        acol = acol_ref[gi]
        hin = carry_ref[g]
        s_all = _dot(wt_ref[gi], ug)
        cr, ci = _cmul(acol[:P, 0:1], acol[P:, 0:1], hin[:P], hin[P:])
        sr = s_all[:P] + jnp.where(lane == 0, cr, 0.0)
        si = s_all[P:] + jnp.where(lane == 0, ci, 0.0)
        for i in range(SCAN_STEPS):
            sh = 1 << i
            pr = jnp.where(lane >= sh, pltpu.roll(sr, sh, axis=1), 0.0)
            pi = jnp.where(lane >= sh, pltpu.roll(si, sh, axis=1), 0.0)
            qr, qi = _cmul(acol[:P, i:i + 1], acol[P:, i:i + 1], pr, pi)
            sr, si = sr + qr, si + qi
        hpr = jnp.where(lane >= 1, pltpu.roll(sr, 1, axis=1), hin[:P])
        hpi = jnp.where(lane >= 1, pltpu.roll(si, 1, axis=1), hin[P:])
        hend = jnp.concatenate([sr[:, nk - 1:nk], si[:, nk - 1:nk]], axis=0)
        carry_ref[g] = hend
        hout_ref[0, 0, gi] = hend
        hprev = jnp.concatenate([hpr, hpi], axis=0).astype(BF16)
        y = _dot(tm_ref[gi, :, 0:CHUNK_W], ug) + _dot(tm_ref[gi, :, CHUNK_W:], hprev)
        yt_ref[:, pl.ds(row0, SSM_GROUP), :] = y.reshape(CHUNK, SSM_GROUP, nk)

    @pl.when(gg == pl.num_programs(2) - 1)
    def _():
        for t in range(CHUNK):
            xs = x_ref[0, :, t * D:(t + 1) * D]
            u = xs * r_ref[t] * g_ref[...]
            v = yt_ref[t].T + d_ref[...] * u
            z_ref[0, :, t * D:(t + 1) * D] = _gelu(v).astype(BF16)


def _s5_mix_prompt(x, pre_g, d_skip, wt, tm, acol):
    B, L, D = x.shape
    nk = 1 << SCAN_STEPS
    blk = nk * CHUNK
    assert L % blk == 0
    NB = L // blk
    gs = 8
    G = N_GROUPS
    x2 = x.reshape(B, L // CHUNK, CHUNK * D)
    z2, hout = pl.pallas_call(
        functools.partial(_s5_mix_prompt_kernel, nk=nk, gs=gs),
        grid=(B, NB, G // gs),
        in_specs=[pl.BlockSpec((1, nk, CHUNK * D), lambda b, n, g: (b, n, 0)),
                  pl.BlockSpec((1, D), lambda b, n, g: (0, 0)),
                  pl.BlockSpec((1, D), lambda b, n, g: (0, 0)),
                  pl.BlockSpec((gs, STATE2, CHUNK_W), lambda b, n, g: (g, 0, 0)),
                  pl.BlockSpec((gs, CHUNK_W, CHUNK_W + STATE2), lambda b, n, g: (g, 0, 0)),
                  pl.BlockSpec((gs, STATE2, ACOLS), lambda b, n, g: (g, 0, 0))],
        out_specs=[pl.BlockSpec((1, nk, CHUNK * D), lambda b, n, g: (b, n, 0)),
                   pl.BlockSpec((1, 1, gs, STATE2, 1), lambda b, n, g: (b, n, g, 0, 0))],
        out_shape=[jax.ShapeDtypeStruct((B, L // CHUNK, CHUNK * D), BF16),
                   jax.ShapeDtypeStruct((B, NB, G, STATE2, 1), F32)],
        scratch_shapes=[pltpu.VMEM((CHUNK, D, nk), BF16),
                        pltpu.VMEM((CHUNK, D, nk), F32),
                        pltpu.VMEM((CHUNK, nk, 1), F32),
                        pltpu.VMEM((G, STATE2, 1), F32)],
        compiler_params=_cparams(("arbitrary", "arbitrary", "arbitrary")),
        name="s5_mix_prompt",
    )(x2, pre_g.reshape(1, D), d_skip.reshape(1, D), wt, tm, acol)
    return z2, hout[:, NB - 1]


def _s5_mix_sample_kernel(x_ref, g_ref, d_ref, h0_ref, wt_ref, tm_ref, acol_ref, z_ref, hout_ref,
                          ut_ref, yt_ref, r_ref, *, nseq, ntok, gs):
    P, D = SSM_STATE, D_MODEL
    half = ntok * SSM_GROUP
    gg = pl.program_id(0)

    @pl.when(gg == 0)
    def _():
        for s in range(ntok):
            xs = x_ref[:, s * D:(s + 1) * D]
            r = _inv_rms(xs)
            r_ref[s] = r
            ut_ref[s] = (xs * r * g_ref[...]).T.astype(BF16)

    for gi in range(gs):
        g = gg * gs + gi
        row0 = pl.multiple_of(g * SSM_GROUP, SSM_GROUP)
        ug = ut_ref[:, pl.ds(row0, SSM_GROUP), :].reshape(half, nseq)
        h0 = h0_ref[gi]
        acol = acol_ref[gi]
        s_all = _dot(wt_ref[gi, :, half:], ug)
        er, ei = _cmul(acol[:P, ACOLS - 1:ACOLS], acol[P:, ACOLS - 1:ACOLS], h0[:P], h0[P:])
        hout_ref[gi] = jnp.concatenate([er + s_all[:P], ei + s_all[P:]], axis=0)
        y = _dot(tm_ref[gi, 0:half, 0:half], ug) + _dot(tm_ref[gi, 0:half, CHUNK_W:], h0.astype(BF16))
        yt_ref[:, pl.ds(row0, SSM_GROUP), :] = y.reshape(ntok, SSM_GROUP, nseq)

    @pl.when(gg == pl.num_programs(0) - 1)
    def _():
        for t in range(ntok):
            xs = x_ref[:, t * D:(t + 1) * D]
            u = xs * r_ref[t] * g_ref[...]
            v = yt_ref[t].T + d_ref[...] * u
            z_ref[:, t * D:(t + 1) * D] = _gelu(v).astype(BF16)


def _s5_mix_sample(x, h0, pre_g, d_skip, wt, tm, acol):
    S, T, D = x.shape
    assert T * 2 == CHUNK
    gs = 8
    G = N_GROUPS
    z2, hout = pl.pallas_call(
        functools.partial(_s5_mix_sample_kernel, nseq=S, ntok=T, gs=gs),
        grid=(G // gs,),
        in_specs=[pl.BlockSpec((S, T * D), lambda g: (0, 0)),
                  pl.BlockSpec((1, D), lambda g: (0, 0)),
                  pl.BlockSpec((1, D), lambda g: (0, 0)),
                  pl.BlockSpec((gs, STATE2, S), lambda g: (g, 0, 0)),
                  pl.BlockSpec((gs, STATE2, CHUNK_W), lambda g: (g, 0, 0)),
                  pl.BlockSpec((gs, CHUNK_W, CHUNK_W + STATE2), lambda g: (g, 0, 0)),
                  pl.BlockSpec((gs, STATE2, ACOLS), lambda g: (g, 0, 0))],
        out_specs=[pl.BlockSpec((S, T * D), lambda g: (0, 0)),
                   pl.BlockSpec((gs, STATE2, S), lambda g: (g, 0, 0))],
        out_shape=[jax.ShapeDtypeStruct((S, T * D), BF16),
                   jax.ShapeDtypeStruct((G, STATE2, S), F32)],
        scratch_shapes=[pltpu.VMEM((T, D, S), BF16),
                        pltpu.VMEM((T, D, S), F32),
                        pltpu.VMEM((T, S, 1), F32)],
        compiler_params=_cparams(("arbitrary",)),
        name="s5_mix_sample",
    )(x.reshape(S, T * D), pre_g.reshape(1, D), d_skip.reshape(1, D), h0, wt, tm, acol)
    return z2, hout


def _tail_kernel(a_ref, w_ref, g_ref, x_ref, o_ref, *, glu):
    y = _dot(a_ref[...], w_ref[...])
    if glu:
        n = y.shape[-1] // 2
        y = y[:, :n] * jax.nn.sigmoid(y[:, n:])
    o_ref[...] = x_ref[...] + y * _inv_rms(y) * g_ref[...]


def _tail(a2, w, g, x2, *, glu, tm):
    R = a2.shape[0]
    K = w.shape[0]
    D = D_MODEL
    n = a2.shape[1] // K
    tm = min(tm, R)
    assert R % tm == 0 and x2.shape == (R, n * D)
    return pl.pallas_call(
        functools.partial(_tail_kernel, glu=glu),
        grid=(R // tm, n),
        in_specs=[pl.BlockSpec((tm, K), lambda i, s: (i, s)),
                  _resident(w.shape),
                  pl.BlockSpec((1, D), lambda i, s: (0, 0)),
                  pl.BlockSpec((tm, D), lambda i, s: (i, s))],
        out_specs=pl.BlockSpec((tm, D), lambda i, s: (i, s)),
        out_shape=jax.ShapeDtypeStruct((R, n * D), F32),
        compiler_params=_cparams(("parallel", "parallel")),
        name="glu_tail" if glu else "oproj_tail",
    )(a2, w, g.reshape(1, D), x2)


def _kvq_kernel(x_ref, gkv_ref, gq_ref, wk_ref, wv_ref, wq_ref, k_ref, v_ref, *q_refs, q_by_head):
    x = x_ref[...]
    xr = x * _inv_rms(x)
    kv_in = (xr * gkv_ref[...]).astype(BF16)
    xn = (xr * gq_ref[...]).astype(BF16)
    k = _dot(kv_in, wk_ref[...])
    v = _dot(kv_in, wv_ref[...])
    q = _dot(xn, wq_ref[...]) * (HEAD_DIM ** -0.5)

    def put_heads(ref, val):
        for h in range(N_HEADS):
            ref[:, h, :] = val[:, h * V_DIM:(h + 1) * V_DIM]

    put_heads(k_ref, k)
    put_heads(v_ref, v)
    if q_by_head:
        put_heads(q_refs[0], q)
    else:
        kb_ref, vb_ref, qb_ref = q_refs
        kb_ref[...] = k.astype(BF16)
        vb_ref[...] = v.astype(BF16)
        qb_ref[...] = q.astype(BF16)


def _kvq(x2, g_kv, g_q, wk, wv, wq, *, tm, q_by_head):
    R, D = x2.shape
    tm = min(tm, R)
    assert R % tm == 0
    tile = pl.BlockSpec((tm, D), lambda i: (i, 0))
    heads = pl.BlockSpec((tm, N_HEADS, V_DIM), lambda i: (i, 0, 0))
    vec = pl.BlockSpec((1, D), lambda i: (0, 0))
    by_head = jax.ShapeDtypeStruct((R, N_HEADS, V_DIM), F32)
    flat16 = jax.ShapeDtypeStruct((R, D), BF16)
    return pl.pallas_call(
        functools.partial(_kvq_kernel, q_by_head=q_by_head),
        grid=(R // tm,),
        in_specs=[tile, vec, vec, _resident(wk.shape), _resident(wv.shape), _resident(wq.shape)],
        out_specs=[heads, heads] + ([heads] if q_by_head else [tile] * 3),
        out_shape=[by_head, by_head] + ([by_head] if q_by_head else [flat16] * 3),
        compiler_params=_cparams(("parallel",)),
        name="kvq_proj",
    )(x2, g_kv.reshape(1, D), g_q.reshape(1, D), wk, wv, wq)


FF_CHUNK = D_FF // 2


def _ffn_cols(c):
    return (slice(c * FF_CHUNK, (c + 1) * FF_CHUNK),
            slice(D_FF + c * FF_CHUNK, D_FF + (c + 1) * FF_CHUNK))


def _ffn_prompt_kernel(x_ref, gpre_ref, gpost_ref, wup_ref, cw_ref, cb_ref, wdn_ref,
                       o_ref, cs_ref, carry_ref, *, tm):
    @pl.when(pl.program_id(1) == 0)
    def _():
        carry_ref[...] = jnp.zeros_like(carry_ref)

    x = x_ref[0]
    xn = (x * _inv_rms(x) * gpre_ref[...]).astype(BF16)
    rows = lax.broadcasted_iota(jnp.int32, (tm, 1), 0)
    f = jnp.zeros((tm, D_MODEL), F32)
    for c in range(D_FF // FF_CHUNK):
        halves = []
        for cols in _ffn_cols(c):
            h = _dot(xn, wup_ref[:, cols])
            c0, c1 = carry_ref[6:7, cols], carry_ref[7:8, cols]
            h1 = jnp.where(rows == 0, c1, pltpu.roll(h, 1, axis=0))
            h2 = jnp.where(rows == 0, c0, jnp.where(rows == 1, c1, pltpu.roll(h, 2, axis=0)))
            halves.append(cb_ref[:, cols] + cw_ref[2:3, cols] * h
                          + cw_ref[1:2, cols] * h1 + cw_ref[0:1, cols] * h2)
            carry_ref[:, cols] = h[tm - 8:tm]
            cs_ref[0, :, cols] = h[tm - 2:tm]
        a = (_gelu(halves[0]) * halves[1]).astype(BF16)
        f = f + _dot(a, wdn_ref[c * FF_CHUNK:(c + 1) * FF_CHUNK, :])
    o_ref[0] = x + f * _inv_rms(f) * gpost_ref[...]


def _ffn_prompt(x, gpre, gpost, wup, cw, cb, wdn, *, tm):
    B, L, D = x.shape
    tm = min(tm, L)
    assert L % tm == 0
    F2 = 2 * D_FF
    vec = pl.BlockSpec((1, D), lambda b, t: (0, 0))
    return pl.pallas_call(
        functools.partial(_ffn_prompt_kernel, tm=tm),
        grid=(B, L // tm),
        in_specs=[pl.BlockSpec((1, tm, D), lambda b, t: (b, t, 0)), vec, vec,
                  _resident(wup.shape), _resident((CONV_W, F2)), _resident((1, F2)),
                  _resident(wdn.shape)],
        out_specs=[pl.BlockSpec((1, tm, D), lambda b, t: (b, t, 0)),
                   pl.BlockSpec((1, CONV_W - 1, F2), lambda b, t: (b, 0, 0))],
        out_shape=[jax.ShapeDtypeStruct((B, L, D), F32),
                   jax.ShapeDtypeStruct((B, CONV_W - 1, F2), F32)],
        scratch_shapes=[pltpu.VMEM((8, F2), F32)],
        compiler_params=_cparams(("arbitrary", "arbitrary")),
        name="ffn_prompt",
    )(x, gpre.reshape(1, D), gpost.reshape(1, D), wup, cw, cb.reshape(1, F2), wdn)


def _ffn_sample_kernel(x_ref, buf_ref, gpre_ref, gpost_ref, wup_ref, cw_ref, cb_ref, wdn_ref,
                       o_ref, cs_ref, *, tm, ntok):
    nsq = tm // ntok
    x = x_ref[...]
    xn = (x * _inv_rms(x) * gpre_ref[...]).astype(BF16)
    sub = lax.broadcasted_iota(jnp.int32, (tm, 1), 0) % ntok
    f = jnp.zeros((tm, D_MODEL), F32)
    for c in range(D_FF // FF_CHUNK):
        halves = []
        for cols in _ffn_cols(c):
            h = _dot(xn, wup_ref[:, cols])
            spread = lambda b: jnp.broadcast_to(b, (nsq, ntok, FF_CHUNK)).reshape(tm, FF_CHUNK)
            b0, b1 = spread(buf_ref[:, 0:1, cols]), spread(buf_ref[:, 1:2, cols])
            h1 = jnp.where(sub == 0, b1, pltpu.roll(h, 1, axis=0))
            h2 = jnp.where(sub == 0, b0, jnp.where(sub == 1, b1, pltpu.roll(h, 2, axis=0)))
            halves.append(cb_ref[:, cols] + cw_ref[2:3, cols] * h
                          + cw_ref[1:2, cols] * h1 + cw_ref[0:1, cols] * h2)
            cs_ref[:, :, cols] = h.reshape(nsq, ntok, FF_CHUNK)[:, ntok - 2:ntok, :]
        a = (_gelu(halves[0]) * halves[1]).astype(BF16)
        f = f + _dot(a, wdn_ref[c * FF_CHUNK:(c + 1) * FF_CHUNK, :])
    o_ref[...] = x + f * _inv_rms(f) * gpost_ref[...]


def _ffn_sample(x, buf, gpre, gpost, wup, cw, cb, wdn, *, tm):
    S, T, D = x.shape
    assert T == 8 and tm % T == 0 and (S * T) % tm == 0
    F2 = 2 * D_FF
    nsq = tm // T
    vec = pl.BlockSpec((1, D), lambda i: (0, 0))
    out, cs = pl.pallas_call(
        functools.partial(_ffn_sample_kernel, tm=tm, ntok=T),
        grid=(S * T // tm,),
        in_specs=[pl.BlockSpec((tm, D), lambda i: (i, 0)),
                  pl.BlockSpec((nsq, CONV_W - 1, F2), lambda i: (i, 0, 0)), vec, vec,
                  _resident(wup.shape), _resident((CONV_W, F2)), _resident((1, F2)),
                  _resident(wdn.shape)],
        out_specs=[pl.BlockSpec((tm, D), lambda i: (i, 0)),
                   pl.BlockSpec((nsq, CONV_W - 1, F2), lambda i: (i, 0, 0))],
        out_shape=[jax.ShapeDtypeStruct((S * T, D), F32),
                   jax.ShapeDtypeStruct((S, CONV_W - 1, F2), F32)],
        compiler_params=_cparams(("parallel",)),
        name="ffn_sample",
    )(x.reshape(S * T, D), buf, gpre.reshape(1, D), gpost.reshape(1, D), wup, cw,
      cb.reshape(1, F2), wdn)
    return out.reshape(S, T, D), cs


def _lam_value(lq1, lk1, lq2, lk2, lam_init):
    return (jnp.exp(jnp.sum(lq1[...] * lk1[...], axis=-1, keepdims=True))
            - jnp.exp(jnp.sum(lq2[...] * lk2[...], axis=-1, keepdims=True)) + lam_init)


def _sub_norm(o, subg, lam_init):
    return o * _inv_rms(o) * subg * (1.0 - lam_init)


def _flash_kernel(qi_ref, ki_ref, q_ref, k_ref, v_ref, lq1, lk1, lq2, lk2, subg_ref, o_ref,
                  m_ref, l_ref, acc_ref, *, tq, lam_init):
    h, step = pl.program_id(1), pl.program_id(2)
    qi, ki = qi_ref[step], ki_ref[step]

    @pl.when(ki == 0)
    def _():
        m_ref[...] = jnp.full_like(m_ref, -jnp.inf)
        l_ref[...] = jnp.zeros_like(l_ref)
        acc_ref[...] = jnp.zeros_like(acc_ref)

    def absorb(diagonal):
        q, k, v = q_ref[...], k_ref[...], v_ref[...]
        col = lax.broadcasted_iota(jnp.int32, (1, V_DIM), 1)
        kpos = lax.broadcasted_iota(jnp.int32, (1, tq), 1) + (ki - qi) * tq
        slope = jnp.exp2(-jnp.full((1, tq), h + 1, jnp.int32).astype(F32))
        bias = slope * kpos.astype(F32)
        if diagonal:
            allowed = kpos <= lax.broadcasted_iota(jnp.int32, (tq, 1), 0)
        for j in range(2):
            qj = jnp.where((col >= HEAD_DIM) == (j == 1), q, jnp.zeros_like(q))
            s = lax.dot_general(qj, k, (((1,), (1,)), ((), ())), preferred_element_type=F32) + bias
            if diagonal:
                s = jnp.where(allowed, s, -jnp.inf)
            m_old = m_ref[j]
            m_new = jnp.maximum(m_old, jnp.max(s, axis=-1, keepdims=True))
            alpha = jnp.exp(m_old - m_new)
            p = jnp.exp(s - m_new)
            l_ref[j] = alpha * l_ref[j] + jnp.sum(p, axis=-1, keepdims=True)
            acc_ref[j] = alpha * acc_ref[j] + _dot(p.astype(BF16), v)
            m_ref[j] = m_new

    @pl.when(ki < qi)
    def _():
        absorb(False)

    @pl.when(ki == qi)
    def _():
        absorb(True)
        lam = _lam_value(lq1, lk1, lq2, lk2, lam_init)
        o = acc_ref[0] / l_ref[0] - lam * (acc_ref[1] / l_ref[1])
        o_ref[...] = _sub_norm(o, subg_ref[...], lam_init).astype(o_ref.dtype)


def _flash_prompt(qb, kb, vb, lq1, lk1, lq2, lk2, subg, *, B, L, tq, lam_init):
    tq = min(tq, L)
    assert L % tq == 0
    nq = L // tq
    pairs = [(i, j) for i in range(nq) for j in range(i + 1)]
    qi_tab = jnp.asarray([p[0] for p in pairs], jnp.int32)
    ki_tab = jnp.asarray([p[1] for p in pairs], jnp.int32)
    lvec = pl.BlockSpec((1, HEAD_DIM), lambda b, h, s, qt, kt: (0, 0))
    grid_spec = pltpu.PrefetchScalarGridSpec(
        num_scalar_prefetch=2,
        grid=(B, N_HEADS, len(pairs)),
        in_specs=[pl.BlockSpec((tq, V_DIM), lambda b, h, s, qt, kt: (b * nq + qt[s], h)),
                  pl.BlockSpec((tq, V_DIM), lambda b, h, s, qt, kt: (b * nq + kt[s], h)),
                  pl.BlockSpec((tq, V_DIM), lambda b, h, s, qt, kt: (b * nq + kt[s], h)),
                  lvec, lvec, lvec, lvec,
                  pl.BlockSpec((1, V_DIM), lambda b, h, s, qt, kt: (0, 0))],
        out_specs=pl.BlockSpec((tq, V_DIM), lambda b, h, s, qt, kt: (b * nq + qt[s], h)),
        scratch_shapes=[pltpu.VMEM((2, tq, 1), F32), pltpu.VMEM((2, tq, 1), F32),
                        pltpu.VMEM((2, tq, V_DIM), F32)])
    return pl.pallas_call(
        functools.partial(_flash_kernel, tq=tq, lam_init=lam_init),
        grid_spec=grid_spec,
        out_shape=jax.ShapeDtypeStruct((B * L, N_HEADS * V_DIM), BF16),
        compiler_params=_cparams(("parallel", "parallel", "arbitrary")),
        name="flash_prompt",
    )(qi_tab, ki_tab, qb, kb, vb, lq1, lk1, lq2, lk2, subg)


def _paged_kernel(pt_ref, q_ref, *refs, ntok, page, npp, past_len, lam_init):
    kc = refs[:npp]
    vc = refs[npp:2 * npp]
    (kn_ref, vn_ref, lq1, lk1, lq2, lk2, subg_ref, o_ref,
     qall_ref, bias_ref, m_ref, l_ref, acc_ref) = refs[2 * npp:]
    H = N_HEADS
    R = 2 * ntok * H
    PW = page * H
    pg = pl.program_id(1)

    row = lax.broadcasted_iota(jnp.int32, (R, 1), 0)
    slope_r = jnp.exp2(-(row % H + 1).astype(F32))
    qi_r = (row // H) % ntok

    def head_bias(n_lanes, key_limit):
        lane = lax.broadcasted_iota(jnp.int32, (1, n_lanes), 1)
        key = lane // H
        ok = (lane % H == row % H) & (key < key_limit)
        return jnp.where(ok, slope_r * (key - qi_r).astype(F32), -jnp.inf)

    @pl.when(pg == 0)
    def _():
        m_ref[...] = jnp.full_like(m_ref, -jnp.inf)
        l_ref[...] = jnp.zeros_like(l_ref)
        acc_ref[...] = jnp.zeros_like(acc_ref)
        q2 = q_ref[...].reshape(ntok * H, V_DIM)
        col = lax.broadcasted_iota(jnp.int32, (1, V_DIM), 1)
        qall_ref[...] = jnp.concatenate([jnp.where(col < HEAD_DIM, q2, 0.0),
                                         jnp.where(col >= HEAD_DIM, q2, 0.0)], axis=0).astype(BF16)
        bias_ref[...] = head_bias(PW, page)

    def absorb(ks, vs, biases):
        qall = qall_ref[...]
        s = [lax.dot_general(qall, k, (((1,), (1,)), ((), ())), preferred_element_type=F32) + b
             for k, b in zip(ks, biases)]
        m_old = m_ref[...]
        m_new = m_old
        for si in s:
            m_new = jnp.maximum(m_new, jnp.max(si, axis=-1, keepdims=True))
        alpha = jnp.exp(m_old - m_new)
        l_new = alpha * l_ref[...]
        acc = alpha * acc_ref[...]
        for si, v in zip(s, vs):
            p = jnp.exp(si - m_new)
            l_new = l_new + jnp.sum(p, axis=-1, keepdims=True)
            acc = acc + _dot(p.astype(BF16), v)
        m_ref[...] = m_new
        l_ref[...] = l_new
        acc_ref[...] = acc

    flat = lambda ref: ref[0].reshape(PW, V_DIM).astype(BF16)
    biases = [bias_ref[...] + slope_r * ((pg * npp + i) * page - past_len).astype(F32) for i in range(npp)]
    absorb([flat(r) for r in kc], [flat(r) for r in vc], biases)

    @pl.when(pg == pl.num_programs(1) - 1)
    def _():
        n_new = ntok * H
        pad = jnp.zeros((R - n_new, V_DIM), F32)
        kn = jnp.concatenate([kn_ref[...].reshape(n_new, V_DIM), pad], axis=0).astype(BF16)
        vn = jnp.concatenate([vn_ref[...].reshape(n_new, V_DIM), pad], axis=0).astype(BF16)
        absorb([kn], [vn], [head_bias(R, jnp.minimum(qi_r + 1, ntok))])
        lam = _lam_value(lq1, lk1, lq2, lk2, lam_init)
        h = R // 2
        o = acc_ref[0:h] / l_ref[0:h] - lam * (acc_ref[h:R] / l_ref[h:R])
        o_ref[...] = _sub_norm(o, subg_ref[...], lam_init).reshape(ntok, H, V_DIM)


def _paged_sample(page_table, q, cache_k, cache_v, k_new, v_new, lq1, lk1, lq2, lk2, subg,
                  *, ntok, npp, lam_init):
    S, n_pages = page_table.shape
    page = cache_k.shape[1]
    H = N_HEADS
    R = 2 * ntok * H
    assert n_pages % npp == 0
    tok = pl.BlockSpec((ntok, H, V_DIM), lambda s, p, pt: (s, 0, 0))
    pgs = [pl.BlockSpec((1, page, H, V_DIM), functools.partial(
        lambda s, p, pt, i: (pt[s, p * npp + i], 0, 0, 0), i=i)) for i in range(npp)]
    lvec = pl.BlockSpec((1, HEAD_DIM), lambda s, p, pt: (0, 0))
    grid_spec = pltpu.PrefetchScalarGridSpec(
        num_scalar_prefetch=1,
        grid=(S, n_pages // npp),
        in_specs=[tok] + pgs + pgs + [tok, tok, lvec, lvec, lvec, lvec,
                                      pl.BlockSpec((1, V_DIM), lambda s, p, pt: (0, 0))],
        out_specs=tok,
        scratch_shapes=[pltpu.VMEM((R, V_DIM), BF16), pltpu.VMEM((R, page * H), F32),
                        pltpu.VMEM((R, 1), F32), pltpu.VMEM((R, 1), F32), pltpu.VMEM((R, V_DIM), F32)])
    return pl.pallas_call(
        functools.partial(_paged_kernel, ntok=ntok, page=page, npp=npp, past_len=n_pages * page,
                          lam_init=lam_init),
        grid_spec=grid_spec,
        out_shape=jax.ShapeDtypeStruct((S * ntok, H, V_DIM), F32),
        compiler_params=_cparams(("arbitrary", "arbitrary")),
        name="paged_sample",
    )(page_table, q, *([cache_k] * npp), *([cache_v] * npp), k_new, v_new, lq1, lk1, lq2, lk2, subg)


def kernel(x_prompt, x_sample, state_ssm_re, state_ssm_im, state_conv, cache_k, cache_v, page_table,
           a_pre_g, a_post_g, ssm_lam_re, ssm_lam_im, ssm_log_dt, ssm_b_re, ssm_b_im, ssm_c_re,
           ssm_c_im, ssm_d, glu_w, kv_norm_g, w_k, w_v, b_pre_g, b_post_g, w_q, lam_q1, lam_k1,
           lam_q2, lam_k2, sub_g, w_o, f_pre_g, f_post_g, w_up, conv_w, conv_b, w_down):
    B, L, D = x_prompt.shape
    S, T, _ = x_sample.shape
    P, G = SSM_STATE, N_GROUPS
    HW = N_HEADS * V_DIM
    st_dtype = state_ssm_re.dtype

    wt, tm_op, acol = _s5_prepare(ssm_lam_re[0], ssm_lam_im[0], ssm_log_dt[0], ssm_b_re[0],
                                  ssm_b_im[0], ssm_c_re[0], ssm_c_im[0])
    glu_b = glu_w[0].astype(BF16)
    zp, hp = _s5_mix_prompt(x_prompt, a_pre_g[0], ssm_d[0], wt, tm_op, acol)
    xp = _tail(zp.reshape(B * L // CHUNK, CHUNK * D), glu_b, a_post_g[0],
               x_prompt.reshape(B * L // CHUNK, CHUNK * D), glu=True, tm=512).reshape(B, L, D)
    h0 = jnp.concatenate([state_ssm_re[0].astype(F32), state_ssm_im[0].astype(F32)], axis=-1)
    zs, hs = _s5_mix_sample(x_sample, h0.transpose(1, 2, 0), a_pre_g[0], ssm_d[0], wt, tm_op, acol)
    xs = _tail(zs, glu_b, a_post_g[0], x_sample.reshape(S, T * D), glu=True, tm=512).reshape(S, T, D)

    hp = hp.reshape(B, G, 2 * P)
    hs = hs.transpose(2, 0, 1)
    ssm_re_p, ssm_im_p = hp[None, ..., :P].astype(st_dtype), hp[None, ..., P:].astype(st_dtype)
    ssm_re_s, ssm_im_s = hs[None, ..., :P].astype(st_dtype), hs[None, ..., P:].astype(st_dtype)

    wup_b, wdn_b = w_up.astype(BF16), w_down.astype(BF16)
    xp, conv_p0 = _ffn_prompt(xp, f_pre_g[0], f_post_g[0], wup_b[0], conv_w[0], conv_b[0], wdn_b[0], tm=256)
    xs, conv_s0 = _ffn_sample(xs, state_conv[0], f_pre_g[0], f_post_g[0], wup_b[0], conv_w[0],
                              conv_b[0], wdn_b[0], tm=256)

    lam_init = 0.8 - 0.6 * math.exp(-0.3 * N_A_LAYERS)
    wk_b, wv_b, wq_b, wo_b = w_k.astype(BF16), w_v.astype(BF16), w_q[0].astype(BF16), w_o[0].astype(BF16)
    lvec = lambda a: a[0].reshape(1, HEAD_DIM).astype(F32)
    lams = (lvec(lam_q1), lvec(lam_k1), lvec(lam_q2), lvec(lam_k2))
    subg = sub_g[0].reshape(1, V_DIM)

    xp2 = xp.reshape(B * L, D)
    kp, vp, kpb, vpb, qpb = _kvq(xp2, kv_norm_g, b_pre_g[0], wk_b, wv_b, wq_b, tm=512, q_by_head=False)
    op = _flash_prompt(qpb, kpb, vpb, *lams, subg, B=B, L=L, tq=512, lam_init=lam_init)
    xp = _tail(op, wo_b, b_post_g[0], xp2, glu=False, tm=512).reshape(B, L, D)

    xs2 = xs.reshape(S * T, D)
    ks, vs, qs = _kvq(xs2, kv_norm_g, b_pre_g[0], wk_b, wv_b, wq_b, tm=512, q_by_head=True)
    os_ = _paged_sample(page_table, qs, cache_k, cache_v, ks, vs, *lams, subg, ntok=T, npp=4,
                        lam_init=lam_init)
    xs = _tail(os_.reshape(S * T, HW).astype(BF16), wo_b, b_post_g[0], xs2, glu=False, tm=512).reshape(S, T, D)

    xp, conv_p1 = _ffn_prompt(xp, f_pre_g[1], f_post_g[1], wup_b[1], conv_w[1], conv_b[1], wdn_b[1], tm=256)
    xs, conv_s1 = _ffn_sample(xs, state_conv[1], f_pre_g[1], f_post_g[1], wup_b[1], conv_w[1],
                              conv_b[1], wdn_b[1], tm=256)

    return (xp, xs, ssm_re_p, ssm_im_p, jnp.stack([conv_p0, conv_p1]),
            kp.reshape(B, L, N_HEADS, 2 * HEAD_DIM), vp.reshape(B, L, N_HEADS, V_DIM),
            ssm_re_s, ssm_im_s, jnp.stack([conv_s0, conv_s1]),
            ks.reshape(S, T, N_HEADS, 2 * HEAD_DIM), vs.reshape(S, T, N_HEADS, V_DIM))
```

```python
import functools
import math

import jax
import jax.numpy as jnp
from jax import lax
from jax.experimental import pallas as pl
from jax.experimental.pallas import tpu as pltpu

D_MODEL = 1024
SSM_GROUP = 16
N_GROUPS = D_MODEL // SSM_GROUP
SSM_STATE = 64
N_HEADS = 8
HEAD_DIM = 64
V_DIM = 2 * HEAD_DIM
D_FF = 2816
CONV_W = 3
NORM_EPS = 1e-6
N_A_LAYERS = 1

CHUNK = 16
STATE2 = 2 * SSM_STATE
CHUNK_W = CHUNK * SSM_GROUP
SCAN_STEPS = 7
ACOLS = 8
HIGHEST = lax.Precision.HIGHEST
BF16 = jnp.bfloat16
F32 = jnp.float32
VMEM_LIMIT = 56 * 1024 * 1024


def _cparams(sem, vmem=VMEM_LIMIT):
    return pltpu.CompilerParams(dimension_semantics=sem, vmem_limit_bytes=vmem)


def _resident(shape):
    zeros = (0,) * len(shape)
    return pl.BlockSpec(shape, lambda *_: zeros, pipeline_mode=pl.Buffered(1))


def _inv_rms(x):
    return lax.rsqrt(jnp.mean(x * x, axis=-1, keepdims=True) + NORM_EPS)


def _gelu(x):
    c = math.sqrt(2.0 / math.pi)
    return x * (0.5 * (1.0 + jnp.tanh(c * (x + 0.044715 * (x * x * x)))))


def _dot(a, b):
    return jnp.dot(a, b, preferred_element_type=F32)


def _s5_prep_kernel(lre_c, lim_c, lre_r, lim_r, ldt, bre, bim, cre, cim,
                    wq_ref, zre_ref, zim_ref, kmat_ref, acol_ref):
    P = SSM_STATE
    dt = jnp.exp(ldt[0])

    def cpow(ar, ai, n):
        m = jnp.exp(n * ar)
        return m * jnp.cos(n * ai), m * jnp.sin(n * ai)

    lr, li = lre_c[0], lim_c[0]
    ar, ai = lr * dt, li * dt
    lbr, lbi = cpow(ar, ai, 1.0)
    den = lr * lr + li * li
    nr, ni = lbr - 1.0, lbi
    cr = (nr * lr + ni * li) / den
    ci = (ni * lr - nr * li) / den
    bbr = cr * bre[0] - ci * bim[0]
    bbi = cr * bim[0] + ci * bre[0]
    for s in range(CHUNK):
        pr, pi = cpow(ar, ai, float(CHUNK - 1 - s))
        wq_ref[0, s, 0:P, :] = pr * bbr - pi * bbi
        wq_ref[0, s, P:2 * P, :] = pr * bbi + pi * bbr
    for i in range(ACOLS):
        n = float(CHUNK * 2 ** i) if i < SCAN_STEPS else float(CHUNK // 2)
        pr, pi = cpow(ar, ai, n)
        acol_ref[0, 0:P, i:i + 1] = pr
        acol_ref[0, P:2 * P, i:i + 1] = pi

    arr, air = lre_r[0] * dt, lim_r[0] * dt
    for m in range(CHUNK + 1):
        pr, pi = cpow(arr, air, float(m))
        zre_ref[0, m] = cre[0] * pr - cim[0] * pi
        zim_ref[0, m] = -(cre[0] * pi + cim[0] * pr)
    zr = zre_ref[0, 0:CHUNK].reshape(CHUNK_W, P)
    zi = zim_ref[0, 0:CHUNK].reshape(CHUNK_W, P)
    kmat_ref[0] = (jnp.dot(zr, bbr, precision=HIGHEST, preferred_element_type=F32)
                   + jnp.dot(zi, bbi, precision=HIGHEST, preferred_element_type=F32))


def _s5_prepare(lam_re, lam_im, log_dt, b_re, b_im, c_re, c_im):
    G, P, C = N_GROUPS, SSM_STATE, SSM_GROUP
    col = lambda a: a.reshape(G, P, 1)
    row = lambda a: a.reshape(G, 1, P)
    spec = lambda *blk: pl.BlockSpec((1,) + blk, lambda g: (g,) + (0,) * len(blk))
    wq, zre, zim, kmat, acol = pl.pallas_call(
        _s5_prep_kernel,
        grid=(G,),
        in_specs=[spec(P, 1), spec(P, 1), spec(1, P), spec(1, P), spec(1, 1),
                  spec(P, C), spec(P, C), spec(C, P), spec(C, P)],
        out_specs=[spec(CHUNK, STATE2, C), spec(CHUNK + 1, C, P), spec(CHUNK + 1, C, P),
                   spec(CHUNK_W, C), spec(STATE2, ACOLS)],
        out_shape=[jax.ShapeDtypeStruct((G, CHUNK, STATE2, C), F32),
                   jax.ShapeDtypeStruct((G, CHUNK + 1, C, P), F32),
                   jax.ShapeDtypeStruct((G, CHUNK + 1, C, P), F32),
                   jax.ShapeDtypeStruct((G, CHUNK_W, C), F32),
                   jax.ShapeDtypeStruct((G, STATE2, ACOLS), F32)],
        compiler_params=_cparams(("arbitrary",)),
        name="s5_prep",
    )(col(lam_re), col(lam_im), row(lam_re), row(lam_im), log_dt.reshape(G, 1, 1),
      b_re, b_im, c_re, c_im)
    wt = wq.transpose(0, 2, 1, 3).reshape(G, STATE2, CHUNK_W)
    k4 = kmat.reshape(G, CHUNK, C, C)
    t_idx = jnp.arange(CHUNK)[:, None] - jnp.arange(CHUNK)[None, :]
    toe = jnp.where((t_idx >= 0)[None, :, :, None, None],
                    k4[:, jnp.clip(t_idx, 0, CHUNK - 1)], 0.0)
    toe = toe.transpose(0, 1, 3, 2, 4).reshape(G, CHUNK_W, CHUNK_W)
    m_op = jnp.concatenate([zre[:, 1:], zim[:, 1:]], axis=-1).reshape(G, CHUNK_W, STATE2)
    tm = jnp.concatenate([toe, m_op], axis=-1)
    return wt.astype(BF16), tm.astype(BF16), acol


def _cmul(ar, ai, br, bi):
    return ar * br - ai * bi, ar * bi + ai * br


def _s5_mix_prompt_kernel(x_ref, g_ref, d_ref, wt_ref, tm_ref, acol_ref, z_ref, hout_ref,
                          ut_ref, yt_ref, r_ref, carry_ref, *, nk, gs):
    P, D = SSM_STATE, D_MODEL
    nb, gg = pl.program_id(1), pl.program_id(2)

    @pl.when(gg == 0)
    def _():
        @pl.when(nb == 0)
        def _():
            carry_ref[...] = jnp.zeros_like(carry_ref)
        for s in range(CHUNK):
            xs = x_ref[0, :, s * D:(s + 1) * D]
            r = _inv_rms(xs)
            r_ref[s] = r
            ut_ref[s] = (xs * r * g_ref[...]).T.astype(BF16)

    lane = lax.broadcasted_iota(jnp.int32, (P, nk), 1)
    for gi in range(gs):
        g = gg * gs + gi
        row0 = pl.multiple_of(g * SSM_GROUP, SSM_GROUP)
        ug = ut_ref[:, pl.ds(row0, SSM_GROUP), :].reshape(CHUNK_W, nk)
        acol = acol_ref[gi]
        hin = carry_ref[g]
        s_all = _dot(wt_ref[gi], ug)
        cr, ci = _cmul(acol[:P, 0:1], acol[P:, 0:1], hin[:P], hin[P:])
        sr = s_all[:P] + jnp.where(lane == 0, cr, 0.0)
        si = s_all[P:] + jnp.where(lane == 0, ci, 0.0)
        for i in range(SCAN_STEPS):
            sh = 1 << i
            pr = jnp.where(lane >= sh, pltpu.roll(sr, sh, axis=1), 0.0)
            pi = jnp.where(lane >= sh, pltpu.roll(si, sh, axis=1), 0.0)
            qr, qi = _cmul(acol[:P, i:i + 1], acol[P:, i:i + 1], pr, pi)
            sr, si = sr + qr, si + qi
        hpr = jnp.where(lane >= 1, pltpu.roll(sr, 1, axis=1), hin[:P])
        hpi = jnp.where(lane >= 1, pltpu.roll(si, 1, axis=1), hin[P:])
        hend = jnp.concatenate([sr[:, nk - 1:nk], si[:, nk - 1:nk]], axis=0)
        carry_ref[g] = hend
        hout_ref[0, 0, gi] = hend
        hprev = jnp.concatenate([hpr, hpi], axis=0).astype(BF16)
        y = _dot(tm_ref[gi, :, 0:CHUNK_W], ug) + _dot(tm_ref[gi, :, CHUNK_W:], hprev)
        yt_ref[:, pl.ds(row0, SSM_GROUP), :] = y.reshape(CHUNK, SSM_GROUP, nk)

    @pl.when(gg == pl.num_programs(2) - 1)
    def _():
        for t in range(CHUNK):
            xs = x_ref[0, :, t * D:(t + 1) * D]
            u = xs * r_ref[t] * g_ref[...]
            v = yt_ref[t].T + d_ref[...] * u
            z_ref[0, :, t * D:(t + 1) * D] = _gelu(v).astype(BF16)


def _s5_mix_prompt(x, pre_g, d_skip, wt, tm, acol):
    B, L, D = x.shape
    nk = 1 << SCAN_STEPS
    blk = nk * CHUNK
    assert L % blk == 0
    NB = L // blk
    gs = 8
    G = N_GROUPS
    x2 = x.reshape(B, L // CHUNK, CHUNK * D)
    z2, hout = pl.pallas_call(
        functools.partial(_s5_mix_prompt_kernel, nk=nk, gs=gs),
        grid=(B, NB, G // gs),
        in_specs=[pl.BlockSpec((1, nk, CHUNK * D), lambda b, n, g: (b, n, 0)),
                  pl.BlockSpec((1, D), lambda b, n, g: (0, 0)),
                  pl.BlockSpec((1, D), lambda b, n, g: (0, 0)),
                  pl.BlockSpec((gs, STATE2, CHUNK_W), lambda b, n, g: (g, 0, 0)),
                  pl.BlockSpec((gs, CHUNK_W, CHUNK_W + STATE2), lambda b, n, g: (g, 0, 0)),
                  pl.BlockSpec((gs, STATE2, ACOLS), lambda b, n, g: (g, 0, 0))],
        out_specs=[pl.BlockSpec((1, nk, CHUNK * D), lambda b, n, g: (b, n, 0)),
                   pl.BlockSpec((1, 1, gs, STATE2, 1), lambda b, n, g: (b, n, g, 0, 0))],
        out_shape=[jax.ShapeDtypeStruct((B, L // CHUNK, CHUNK * D), BF16),
                   jax.ShapeDtypeStruct((B, NB, G, STATE2, 1), F32)],
        scratch_shapes=[pltpu.VMEM((CHUNK, D, nk), BF16),
                        pltpu.VMEM((CHUNK, D, nk), F32),
                        pltpu.VMEM((CHUNK, nk, 1), F32),
                        pltpu.VMEM((G, STATE2, 1), F32)],
        compiler_params=_cparams(("arbitrary", "arbitrary", "arbitrary")),
        name="s5_mix_prompt",
    )(x2, pre_g.reshape(1, D), d_skip.reshape(1, D), wt, tm, acol)
    return z2, hout[:, NB - 1]


def _s5_mix_sample_kernel(x_ref, g_ref, d_ref, h0_ref, wt_ref, tm_ref, acol_ref, z_ref, hout_ref,
                          ut_ref, yt_ref, r_ref, *, nseq, ntok, gs):
    P, D = SSM_STATE, D_MODEL
    half = ntok * SSM_GROUP
    gg = pl.program_id(0)

    @pl.when(gg == 0)
    def _():
        for s in range(ntok):
            xs = x_ref[:, s * D:(s + 1) * D]
            r = _inv_rms(xs)
            r_ref[s] = r
            ut_ref[s] = (xs * r * g_ref[...]).T.astype(BF16)

    for gi in range(gs):
        g = gg * gs + gi
        row0 = pl.multiple_of(g * SSM_GROUP, SSM_GROUP)
        ug = ut_ref[:, pl.ds(row0, SSM_GROUP), :].reshape(half, nseq)
        h0 = h0_ref[gi]
        acol = acol_ref[gi]
        s_all = _dot(wt_ref[gi, :, half:], ug)
        er, ei = _cmul(acol[:P, ACOLS - 1:ACOLS], acol[P:, ACOLS - 1:ACOLS], h0[:P], h0[P:])
        hout_ref[gi] = jnp.concatenate([er + s_all[:P], ei + s_all[P:]], axis=0)
        y = _dot(tm_ref[gi, 0:half, 0:half], ug) + _dot(tm_ref[gi, 0:half, CHUNK_W:], h0.astype(BF16))
        yt_ref[:, pl.ds(row0, SSM_GROUP), :] = y.reshape(ntok, SSM_GROUP, nseq)

    @pl.when(gg == pl.num_programs(0) - 1)
    def _():
        for t in range(ntok):
            xs = x_ref[:, t * D:(t + 1) * D]
            u = xs * r_ref[t] * g_ref[...]
            v = yt_ref[t].T + d_ref[...] * u
            z_ref[:, t * D:(t + 1) * D] = _gelu(v).astype(BF16)


def _s5_mix_sample(x, h0, pre_g, d_skip, wt, tm, acol):
    S, T, D = x.shape
    assert T * 2 == CHUNK
    gs = 8
    G = N_GROUPS
    z2, hout = pl.pallas_call(
        functools.partial(_s5_mix_sample_kernel, nseq=S, ntok=T, gs=gs),
        grid=(G // gs,),
        in_specs=[pl.BlockSpec((S, T * D), lambda g: (0, 0)),
                  pl.BlockSpec((1, D), lambda g: (0, 0)),
                  pl.BlockSpec((1, D), lambda g: (0, 0)),
                  pl.BlockSpec((gs, STATE2, S), lambda g: (g, 0, 0)),
                  pl.BlockSpec((gs, STATE2, CHUNK_W), lambda g: (g, 0, 0)),
                  pl.BlockSpec((gs, CHUNK_W, CHUNK_W + STATE2), lambda g: (g, 0, 0)),
                  pl.BlockSpec((gs, STATE2, ACOLS), lambda g: (g, 0, 0))],
        out_specs=[pl.BlockSpec((S, T * D), lambda g: (0, 0)),
                   pl.BlockSpec((gs, STATE2, S), lambda g: (g, 0, 0))],
        out_shape=[jax.ShapeDtypeStruct((S, T * D), BF16),
                   jax.ShapeDtypeStruct((G, STATE2, S), F32)],
        scratch_shapes=[pltpu.VMEM((T, D, S), BF16),
                        pltpu.VMEM((T, D, S), F32),
                        pltpu.VMEM((T, S, 1), F32)],
        compiler_params=_cparams(("arbitrary",)),
        name="s5_mix_sample",
    )(x.reshape(S, T * D), pre_g.reshape(1, D), d_skip.reshape(1, D), h0, wt, tm, acol)
    return z2, hout


def _tail_kernel(a_ref, w_ref, g_ref, x_ref, o_ref, *, glu):
    y = _dot(a_ref[...], w_ref[...])
    if glu:
        n = y.shape[-1] // 2
        y = y[:, :n] * jax.nn.sigmoid(y[:, n:])
    o_ref[...] = x_ref[...] + y * _inv_rms(y) * g_ref[...]


def _tail(a2, w, g, x2, *, glu, tm):
    R = a2.shape[0]
    K = w.shape[0]
    D = D_MODEL
    n = a2.shape[1] // K
    tm = min(tm, R)
    assert R % tm == 0 and x2.shape == (R, n * D)
    return pl.pallas_call(
        functools.partial(_tail_kernel, glu=glu),
        grid=(R // tm, n),
        in_specs=[pl.BlockSpec((tm, K), lambda i, s: (i, s)),
                  _resident(w.shape),
                  pl.BlockSpec((1, D), lambda i, s: (0, 0)),
                  pl.BlockSpec((tm, D), lambda i, s: (i, s))],
        out_specs=pl.BlockSpec((tm, D), lambda i, s: (i, s)),
        out_shape=jax.ShapeDtypeStruct((R, n * D), F32),
        compiler_params=_cparams(("parallel", "parallel")),
        name="glu_tail" if glu else "oproj_tail",
    )(a2, w, g.reshape(1, D), x2)


def _kvq_kernel(x_ref, gkv_ref, gq_ref, wk_ref, wv_ref, wq_ref, k_ref, v_ref, *q_refs, by_head, q_scale):
    x = x_ref[...]
    xr = x * _inv_rms(x)
    kv_in = (xr * gkv_ref[...]).astype(BF16)
    xn = (xr * gq_ref[...]).astype(BF16)
    k = _dot(kv_in, wk_ref[...])
    v = _dot(kv_in, wv_ref[...])
    q = _dot(xn, wq_ref[...]) * q_scale
    if by_head:
        for ref, val in ((k_ref, k), (v_ref, v), (q_refs[0], q)):
            for h in range(N_HEADS):
                ref[:, h, :] = val[:, h * V_DIM:(h + 1) * V_DIM]
    else:
        kb_ref, vb_ref, qb_ref = q_refs
        k_ref[...] = k
        v_ref[...] = v
        kb_ref[...] = k.astype(BF16)
        vb_ref[...] = v.astype(BF16)
        qb_ref[...] = q.astype(BF16)


def _kvq(x2, g_kv, g_q, wk, wv, wq, *, tm, by_head, q_scale):
    R, D = x2.shape
    tm = min(tm, R)
    assert R % tm == 0
    tile = pl.BlockSpec((tm, D), lambda i: (i, 0))
    heads = pl.BlockSpec((tm, N_HEADS, V_DIM), lambda i: (i, 0, 0))
    vec = pl.BlockSpec((1, D), lambda i: (0, 0))
    head32 = jax.ShapeDtypeStruct((R, N_HEADS, V_DIM), F32)
    flat32 = jax.ShapeDtypeStruct((R, D), F32)
    flat16 = jax.ShapeDtypeStruct((R, D), BF16)
    return pl.pallas_call(
        functools.partial(_kvq_kernel, by_head=by_head, q_scale=q_scale),
        grid=(R // tm,),
        in_specs=[tile, vec, vec, _resident(wk.shape), _resident(wv.shape), _resident(wq.shape)],
        out_specs=[heads] * 3 if by_head else [tile] * 5,
        out_shape=[head32] * 3 if by_head else [flat32, flat32, flat16, flat16, flat16],
        compiler_params=_cparams(("parallel",)),
        name="kvq_proj",
    )(x2, g_kv.reshape(1, D), g_q.reshape(1, D), wk, wv, wq)


FF_CHUNK = D_FF // 2


def _ffn_cols(c):
    return (slice(c * FF_CHUNK, (c + 1) * FF_CHUNK),
            slice(D_FF + c * FF_CHUNK, D_FF + (c + 1) * FF_CHUNK))


def _ffn_prompt_kernel(x_ref, gpre_ref, gpost_ref, wup_ref, cw_ref, cb_ref, wdn_ref,
                       o_ref, cs_ref, carry_ref, *, tm):
    @pl.when(pl.program_id(1) == 0)
    def _():
        carry_ref[...] = jnp.zeros_like(carry_ref)

    x = x_ref[0]
    xn = (x * _inv_rms(x) * gpre_ref[...]).astype(BF16)
    rows = lax.broadcasted_iota(jnp.int32, (tm, 1), 0)
    f = jnp.zeros((tm, D_MODEL), F32)
    for c in range(D_FF // FF_CHUNK):
        halves = []
        for cols in _ffn_cols(c):
            h = _dot(xn, wup_ref[:, cols])
            c0, c1 = carry_ref[6:7, cols], carry_ref[7:8, cols]
            h1 = jnp.where(rows == 0, c1, pltpu.roll(h, 1, axis=0))
            h2 = jnp.where(rows == 0, c0, jnp.where(rows == 1, c1, pltpu.roll(h, 2, axis=0)))
            halves.append(cb_ref[:, cols] + cw_ref[2:3, cols] * h
                          + cw_ref[1:2, cols] * h1 + cw_ref[0:1, cols] * h2)
            carry_ref[:, cols] = h[tm - 8:tm]
            cs_ref[0, :, cols] = h[tm - 2:tm]
        a = (_gelu(halves[0]) * halves[1]).astype(BF16)
        f = f + _dot(a, wdn_ref[c * FF_CHUNK:(c + 1) * FF_CHUNK, :])
    o_ref[0] = x + f * _inv_rms(f) * gpost_ref[...]


def _ffn_prompt(x, gpre, gpost, wup, cw, cb, wdn, *, tm):
    B, L, D = x.shape
    tm = min(tm, L)
    assert L % tm == 0
    F2 = 2 * D_FF
    vec = pl.BlockSpec((1, D), lambda b, t: (0, 0))
    return pl.pallas_call(
        functools.partial(_ffn_prompt_kernel, tm=tm),
        grid=(B, L // tm),
        in_specs=[pl.BlockSpec((1, tm, D), lambda b, t: (b, t, 0)), vec, vec,
                  _resident(wup.shape), _resident((CONV_W, F2)), _resident((1, F2)),
                  _resident(wdn.shape)],
        out_specs=[pl.BlockSpec((1, tm, D), lambda b, t: (b, t, 0)),
                   pl.BlockSpec((1, CONV_W - 1, F2), lambda b, t: (b, 0, 0))],
        out_shape=[jax.ShapeDtypeStruct((B, L, D), F32),
                   jax.ShapeDtypeStruct((B, CONV_W - 1, F2), F32)],
        scratch_shapes=[pltpu.VMEM((8, F2), F32)],
        compiler_params=_cparams(("arbitrary", "arbitrary")),
        name="ffn_prompt",
    )(x, gpre.reshape(1, D), gpost.reshape(1, D), wup, cw, cb.reshape(1, F2), wdn)


def _ffn_sample_kernel(x_ref, buf_ref, gpre_ref, gpost_ref, wup_ref, cw_ref, cb_ref, wdn_ref,
                       o_ref, cs_ref, *, tm, ntok):
    nsq = tm // ntok
    x = x_ref[...]
    xn = (x * _inv_rms(x) * gpre_ref[...]).astype(BF16)
    sub = lax.broadcasted_iota(jnp.int32, (tm, 1), 0) % ntok
    f = jnp.zeros((tm, D_MODEL), F32)
    for c in range(D_FF // FF_CHUNK):
        halves = []
        for cols in _ffn_cols(c):
            h = _dot(xn, wup_ref[:, cols])
            spread = lambda b: jnp.broadcast_to(b, (nsq, ntok, FF_CHUNK)).reshape(tm, FF_CHUNK)
            b0, b1 = spread(buf_ref[:, 0:1, cols]), spread(buf_ref[:, 1:2, cols])
            h1 = jnp.where(sub == 0, b1, pltpu.roll(h, 1, axis=0))
            h2 = jnp.where(sub == 0, b0, jnp.where(sub == 1, b1, pltpu.roll(h, 2, axis=0)))
            halves.append(cb_ref[:, cols] + cw_ref[2:3, cols] * h
                          + cw_ref[1:2, cols] * h1 + cw_ref[0:1, cols] * h2)
            cs_ref[:, :, cols] = h.reshape(nsq, ntok, FF_CHUNK)[:, ntok - 2:ntok, :]
        a = (_gelu(halves[0]) * halves[1]).astype(BF16)
        f = f + _dot(a, wdn_ref[c * FF_CHUNK:(c + 1) * FF_CHUNK, :])
    o_ref[...] = x + f * _inv_rms(f) * gpost_ref[...]


def _ffn_sample(x, buf, gpre, gpost, wup, cw, cb, wdn, *, tm):
    S, T, D = x.shape
    assert T == 8 and tm % T == 0 and (S * T) % tm == 0
    F2 = 2 * D_FF
    nsq = tm // T
    vec = pl.BlockSpec((1, D), lambda i: (0, 0))
    out, cs = pl.pallas_call(
        functools.partial(_ffn_sample_kernel, tm=tm, ntok=T),
        grid=(S * T // tm,),
        in_specs=[pl.BlockSpec((tm, D), lambda i: (i, 0)),
                  pl.BlockSpec((nsq, CONV_W - 1, F2), lambda i: (i, 0, 0)), vec, vec,
                  _resident(wup.shape), _resident((CONV_W, F2)), _resident((1, F2)),
                  _resident(wdn.shape)],
        out_specs=[pl.BlockSpec((tm, D), lambda i: (i, 0)),
                   pl.BlockSpec((nsq, CONV_W - 1, F2), lambda i: (i, 0, 0))],
        out_shape=[jax.ShapeDtypeStruct((S * T, D), F32),
                   jax.ShapeDtypeStruct((S, CONV_W - 1, F2), F32)],
        compiler_params=_cparams(("parallel",)),
        name="ffn_sample",
    )(x.reshape(S * T, D), buf, gpre.reshape(1, D), gpost.reshape(1, D), wup, cw,
      cb.reshape(1, F2), wdn)
    return out.reshape(S, T, D), cs


def _lam_value(lq1, lk1, lq2, lk2, lam_init):
    return (jnp.exp(jnp.sum(lq1[...] * lk1[...], axis=-1, keepdims=True))
            - jnp.exp(jnp.sum(lq2[...] * lk2[...], axis=-1, keepdims=True)) + lam_init)


def _sub_norm(o, subg, lam_init):
    return o * _inv_rms(o) * subg * (1.0 - lam_init)


LOG2E = math.log2(math.e)
FLASH_ROWS = 256
POS_SPLIT = 32


def _flash_kernel(qi_ref, ki_ref, q_ref, k_ref, v_ref, lq1, lk1, lq2, lk2, subg_ref, o_ref,
                  q2_ref, kf_ref, m_ref, acc_ref, *, tq, lam_init):
    h, step = pl.program_id(1), pl.program_id(2)
    qi, ki = qi_ref[step], ki_ref[step]
    RQ = FLASH_ROWS
    n_chunks = tq // RQ
    lane = lax.broadcasted_iota(jnp.int32, (1, V_DIM), 1)
    slope2 = jnp.exp2(-jnp.full((1, V_DIM), h + 1, jnp.int32).astype(F32)) * LOG2E

    @pl.when(ki == 0)
    def _():
        m_ref[...] = jnp.full_like(m_ref, -jnp.inf)
        acc_ref[...] = jnp.zeros_like(acc_ref)
        kidx = lax.broadcasted_iota(jnp.int32, (tq, 1), 0)
        a, b = (kidx // POS_SPLIT).astype(F32), (kidx % POS_SPLIT).astype(F32)
        kf_ref[...] = jnp.where(lane < 3, a, jnp.where(lane < 6, b, 0.0)).astype(BF16)
        c0 = slope2.astype(BF16).astype(F32)
        c1 = (slope2 - c0).astype(BF16).astype(F32)
        c2 = (slope2 - c0 - c1).astype(BF16).astype(F32)
        part = jnp.where(lane % 3 == 0, c0, jnp.where(lane % 3 == 1, c1, c2))
        qf = jnp.where(lane < 3, POS_SPLIT * part, jnp.where(lane < 6, part, 0.0)).astype(BF16)
        qf = jnp.broadcast_to(qf, (RQ, V_DIM))
        for c_i in range(n_chunks):
            q = q_ref[c_i * RQ:(c_i + 1) * RQ, :]
            zero = jnp.zeros_like(q)
            q2_ref[c_i, 0:RQ, 0:V_DIM] = jnp.where(lane < HEAD_DIM, q, zero)
            q2_ref[c_i, RQ:2 * RQ, 0:V_DIM] = jnp.where(lane >= HEAD_DIM, q, zero)
            q2_ref[c_i, 0:RQ, V_DIM:] = qf
            q2_ref[c_i, RQ:2 * RQ, V_DIM:] = qf

    def absorb(diagonal):
        k_aug = jnp.concatenate([k_ref[...], kf_ref[...]], axis=1)
        v_aug = jnp.concatenate([v_ref[...], jnp.ones((tq, V_DIM), BF16)], axis=1)
        shift = slope2 * ((ki - qi) * tq).astype(F32)
        for c_i in range(n_chunks):
            nkeys = (c_i + 1) * RQ if diagonal else tq
            s = lax.dot_general(q2_ref[c_i], k_aug[0:nkeys], (((1,), (1,)), ((), ())),
                                preferred_element_type=F32)
            if diagonal:
                kcol = lax.broadcasted_iota(jnp.int32, (1, RQ), 1)
                qrow = lax.broadcasted_iota(jnp.int32, (2 * RQ, 1), 0) % RQ
                tail = jnp.where(kcol <= qrow, s[:, nkeys - RQ:], -jnp.inf)
                s = tail if nkeys == RQ else jnp.concatenate([s[:, 0:nkeys - RQ], tail], axis=1)
            m_old = m_ref[c_i]
            m_new = jnp.maximum(m_old, jnp.max(s, axis=-1, keepdims=True) + shift)
            alpha = jnp.exp2(m_old - m_new)
            p = jnp.exp2(s - pltpu.repeat(m_new - shift, nkeys // V_DIM, axis=1))
            pv = _dot(p.astype(BF16), v_aug[0:nkeys])
            acc_ref[c_i] = pltpu.repeat(alpha, 2, axis=1) * acc_ref[c_i] + pv
            m_ref[c_i] = m_new

    @pl.when(ki < qi)
    def _():
        absorb(False)

    @pl.when(ki == qi)
    def _():
        absorb(True)
        lam = _lam_value(lq1, lk1, lq2, lk2, lam_init)
        for c_i in range(n_chunks):
            acc = acc_ref[c_i]
            o = (acc[0:RQ, 0:V_DIM] / acc[0:RQ, V_DIM:]
                 - lam * (acc[RQ:, 0:V_DIM] / acc[RQ:, V_DIM:]))
            o_ref[c_i * RQ:(c_i + 1) * RQ, :] = _sub_norm(o, subg_ref[...], lam_init).astype(o_ref.dtype)


def _flash_prompt(qb, kb, vb, lq1, lk1, lq2, lk2, subg, *, B, L, tq, lam_init):
    tq = min(tq, L)
    assert L % tq == 0 and tq % FLASH_ROWS == 0 and tq <= POS_SPLIT * POS_SPLIT
    nq = L // tq
    n_chunks = tq // FLASH_ROWS
    pairs = [(i, j) for i in range(nq) for j in range(i + 1)]
    qi_tab = jnp.asarray([p[0] for p in pairs], jnp.int32)
    ki_tab = jnp.asarray([p[1] for p in pairs], jnp.int32)
    lvec = pl.BlockSpec((1, HEAD_DIM), lambda b, h, s, qt, kt: (0, 0))
    grid_spec = pltpu.PrefetchScalarGridSpec(
        num_scalar_prefetch=2,
        grid=(B, N_HEADS, len(pairs)),
        in_specs=[pl.BlockSpec((tq, V_DIM), lambda b, h, s, qt, kt: (b * nq + qt[s], h)),
                  pl.BlockSpec((tq, V_DIM), lambda b, h, s, qt, kt: (b * nq + kt[s], h)),
                  pl.BlockSpec((tq, V_DIM), lambda b, h, s, qt, kt: (b * nq + kt[s], h)),
                  lvec, lvec, lvec, lvec,
                  pl.BlockSpec((1, V_DIM), lambda b, h, s, qt, kt: (0, 0))],
        out_specs=pl.BlockSpec((tq, V_DIM), lambda b, h, s, qt, kt: (b * nq + qt[s], h)),
        scratch_shapes=[pltpu.VMEM((n_chunks, 2 * FLASH_ROWS, 2 * V_DIM), BF16),
                        pltpu.VMEM((tq, V_DIM), BF16),
                        pltpu.VMEM((n_chunks, 2 * FLASH_ROWS, V_DIM), F32),
                        pltpu.VMEM((n_chunks, 2 * FLASH_ROWS, 2 * V_DIM), F32)])
    return pl.pallas_call(
        functools.partial(_flash_kernel, tq=tq, lam_init=lam_init),
        grid_spec=grid_spec,
        out_shape=jax.ShapeDtypeStruct((B * L, N_HEADS * V_DIM), BF16),
        compiler_params=_cparams(("parallel", "parallel", "arbitrary")),
        name="flash_prompt",
    )(qi_tab, ki_tab, qb, kb, vb, lq1, lk1, lq2, lk2, subg)


def _paged_kernel(pt_ref, q_ref, *refs, ntok, page, npp, past_len, lam_init):
    kc = refs[:npp]
    vc = refs[npp:2 * npp]
    (kn_ref, vn_ref, lq1, lk1, lq2, lk2, subg_ref, o_ref,
     qall_ref, bias_ref, m_ref, l_ref, acc_ref) = refs[2 * npp:]
    H = N_HEADS
    R = 2 * ntok * H
    PW = page * H
    pg = pl.program_id(1)

    row = lax.broadcasted_iota(jnp.int32, (R, 1), 0)
    slope_r = jnp.exp2(-(row % H + 1).astype(F32))
    qi_r = (row // H) % ntok

    def head_bias(n_lanes, key_limit):
        lane = lax.broadcasted_iota(jnp.int32, (1, n_lanes), 1)
        key = lane // H
        ok = (lane % H == row % H) & (key < key_limit)
        return jnp.where(ok, slope_r * (key - qi_r).astype(F32), -jnp.inf)

    @pl.when(pg == 0)
    def _():
        m_ref[...] = jnp.full_like(m_ref, -jnp.inf)
        l_ref[...] = jnp.zeros_like(l_ref)
        acc_ref[...] = jnp.zeros_like(acc_ref)
        q2 = q_ref[...].reshape(ntok * H, V_DIM)
        col = lax.broadcasted_iota(jnp.int32, (1, V_DIM), 1)
        qall_ref[...] = jnp.concatenate([jnp.where(col < HEAD_DIM, q2, 0.0),
                                         jnp.where(col >= HEAD_DIM, q2, 0.0)], axis=0).astype(BF16)
        bias_ref[...] = head_bias(PW, page)

    def absorb(ks, vs, biases):
        qall = qall_ref[...]
        s = [lax.dot_general(qall, k, (((1,), (1,)), ((), ())), preferred_element_type=F32) + b
             for k, b in zip(ks, biases)]
        m_old = m_ref[...]
        m_new = m_old
        for si in s:
            m_new = jnp.maximum(m_new, jnp.max(si, axis=-1, keepdims=True))
        alpha = jnp.exp(m_old - m_new)
        l_new = alpha * l_ref[...]
        acc = alpha * acc_ref[...]
        for si, v in zip(s, vs):
            p = jnp.exp(si - m_new)
            l_new = l_new + jnp.sum(p, axis=-1, keepdims=True)
            acc = acc + _dot(p.astype(BF16), v)
        m_ref[...] = m_new
        l_ref[...] = l_new
        acc_ref[...] = acc

    flat = lambda ref: ref[0].reshape(PW, V_DIM).astype(BF16)
    biases = [bias_ref[...] + slope_r * ((pg * npp + i) * page - past_len).astype(F32) for i in range(npp)]
    absorb([flat(r) for r in kc], [flat(r) for r in vc], biases)

    @pl.when(pg == pl.num_programs(1) - 1)
    def _():
        n_new = ntok * H
        pad = jnp.zeros((R - n_new, V_DIM), F32)
        kn = jnp.concatenate([kn_ref[...].reshape(n_new, V_DIM), pad], axis=0).astype(BF16)
        vn = jnp.concatenate([vn_ref[...].reshape(n_new, V_DIM), pad], axis=0).astype(BF16)
        absorb([kn], [vn], [head_bias(R, jnp.minimum(qi_r + 1, ntok))])
        lam = _lam_value(lq1, lk1, lq2, lk2, lam_init)
        h = R // 2
        o = acc_ref[0:h] / l_ref[0:h] - lam * (acc_ref[h:R] / l_ref[h:R])
        o_ref[...] = _sub_norm(o, subg_ref[...], lam_init).reshape(ntok, H, V_DIM)


def _paged_sample(page_table, q, cache_k, cache_v, k_new, v_new, lq1, lk1, lq2, lk2, subg,
                  *, ntok, npp, lam_init):
    S, n_pages = page_table.shape
    page = cache_k.shape[1]
    H = N_HEADS
    R = 2 * ntok * H
    assert n_pages % npp == 0
    tok = pl.BlockSpec((ntok, H, V_DIM), lambda s, p, pt: (s, 0, 0))
    pgs = [pl.BlockSpec((1, page, H, V_DIM), functools.partial(
        lambda s, p, pt, i: (pt[s, p * npp + i], 0, 0, 0), i=i)) for i in range(npp)]
    lvec = pl.BlockSpec((1, HEAD_DIM), lambda s, p, pt: (0, 0))
    grid_spec = pltpu.PrefetchScalarGridSpec(
        num_scalar_prefetch=1,
        grid=(S, n_pages // npp),
        in_specs=[tok] + pgs + pgs + [tok, tok, lvec, lvec, lvec, lvec,
                                      pl.BlockSpec((1, V_DIM), lambda s, p, pt: (0, 0))],
        out_specs=tok,
        scratch_shapes=[pltpu.VMEM((R, V_DIM), BF16), pltpu.VMEM((R, page * H), F32),
                        pltpu.VMEM((R, 1), F32), pltpu.VMEM((R, 1), F32), pltpu.VMEM((R, V_DIM), F32)])
    return pl.pallas_call(
        functools.partial(_paged_kernel, ntok=ntok, page=page, npp=npp, past_len=n_pages * page,
                          lam_init=lam_init),
        grid_spec=grid_spec,
        out_shape=jax.ShapeDtypeStruct((S * ntok, H, V_DIM), F32),
        compiler_params=_cparams(("arbitrary", "arbitrary")),
        name="paged_sample",
    )(page_table, q, *([cache_k] * npp), *([cache_v] * npp), k_new, v_new, lq1, lk1, lq2, lk2, subg)


def kernel(x_prompt, x_sample, state_ssm_re, state_ssm_im, state_conv, cache_k, cache_v, page_table,
           a_pre_g, a_post_g, ssm_lam_re, ssm_lam_im, ssm_log_dt, ssm_b_re, ssm_b_im, ssm_c_re,
           ssm_c_im, ssm_d, glu_w, kv_norm_g, w_k, w_v, b_pre_g, b_post_g, w_q, lam_q1, lam_k1,
           lam_q2, lam_k2, sub_g, w_o, f_pre_g, f_post_g, w_up, conv_w, conv_b, w_down):
    B, L, D = x_prompt.shape
    S, T, _ = x_sample.shape
    P, G = SSM_STATE, N_GROUPS
    HW = N_HEADS * V_DIM
    st_dtype = state_ssm_re.dtype

    wt, tm_op, acol = _s5_prepare(ssm_lam_re[0], ssm_lam_im[0], ssm_log_dt[0], ssm_b_re[0],
                                  ssm_b_im[0], ssm_c_re[0], ssm_c_im[0])
    glu_b = glu_w[0].astype(BF16)
    zp, hp = _s5_mix_prompt(x_prompt, a_pre_g[0], ssm_d[0], wt, tm_op, acol)
    xp = _tail(zp.reshape(B * L // CHUNK, CHUNK * D), glu_b, a_post_g[0],
               x_prompt.reshape(B * L // CHUNK, CHUNK * D), glu=True, tm=512).reshape(B, L, D)
    h0 = jnp.concatenate([state_ssm_re[0].astype(F32), state_ssm_im[0].astype(F32)], axis=-1)
    zs, hs = _s5_mix_sample(x_sample, h0.transpose(1, 2, 0), a_pre_g[0], ssm_d[0], wt, tm_op, acol)
    xs = _tail(zs, glu_b, a_post_g[0], x_sample.reshape(S, T * D), glu=True, tm=512).reshape(S, T, D)

    hp = hp.reshape(B, G, 2 * P)
    hs = hs.transpose(2, 0, 1)
    ssm_re_p, ssm_im_p = hp[None, ..., :P].astype(st_dtype), hp[None, ..., P:].astype(st_dtype)
    ssm_re_s, ssm_im_s = hs[None, ..., :P].astype(st_dtype), hs[None, ..., P:].astype(st_dtype)

    wup_b, wdn_b = w_up.astype(BF16), w_down.astype(BF16)
    xp, conv_p0 = _ffn_prompt(xp, f_pre_g[0], f_post_g[0], wup_b[0], conv_w[0], conv_b[0], wdn_b[0], tm=256)
    xs, conv_s0 = _ffn_sample(xs, state_conv[0], f_pre_g[0], f_post_g[0], wup_b[0], conv_w[0],
                              conv_b[0], wdn_b[0], tm=256)

    lam_init = 0.8 - 0.6 * math.exp(-0.3 * N_A_LAYERS)
    wk_b, wv_b, wq_b, wo_b = w_k.astype(BF16), w_v.astype(BF16), w_q[0].astype(BF16), w_o[0].astype(BF16)
    lvec = lambda a: a[0].reshape(1, HEAD_DIM).astype(F32)
    lams = (lvec(lam_q1), lvec(lam_k1), lvec(lam_q2), lvec(lam_k2))
    subg = sub_g[0].reshape(1, V_DIM)

    xp2 = xp.reshape(B * L, D)
    kp, vp, kpb, vpb, qpb = _kvq(xp2, kv_norm_g, b_pre_g[0], wk_b, wv_b, wq_b, tm=512, by_head=False,
                                 q_scale=LOG2E * HEAD_DIM ** -0.5)
    op = _flash_prompt(qpb, kpb, vpb, *lams, subg, B=B, L=L, tq=1024, lam_init=lam_init)
    xp = _tail(op, wo_b, b_post_g[0], xp2, glu=False, tm=512).reshape(B, L, D)

    xs2 = xs.reshape(S * T, D)
    ks, vs, qs = _kvq(xs2, kv_norm_g, b_pre_g[0], wk_b, wv_b, wq_b, tm=512, by_head=True,
                      q_scale=HEAD_DIM ** -0.5)
    os_ = _paged_sample(page_table, qs, cache_k, cache_v, ks, vs, *lams, subg, ntok=T, npp=4,
                        lam_init=lam_init)
    xs = _tail(os_.reshape(S * T, HW).astype(BF16), wo_b, b_post_g[0], xs2, glu=False, tm=512).reshape(S, T, D)

    xp, conv_p1 = _ffn_prompt(xp, f_pre_g[1], f_post_g[1], wup_b[1], conv_w[1], conv_b[1], wdn_b[1], tm=256)
    xs, conv_s1 = _ffn_sample(xs, state_conv[1], f_pre_g[1], f_post_g[1], wup_b[1], conv_w[1],
                              conv_b[1], wdn_b[1], tm=256)

    return (xp, xs, ssm_re_p, ssm_im_p, jnp.stack([conv_p0, conv_p1]),
            kp.reshape(B, L, N_HEADS, 2 * HEAD_DIM), vp.reshape(B, L, N_HEADS, V_DIM),
            ssm_re_s, ssm_im_s, jnp.stack([conv_s0, conv_s1]),
            ks.reshape(S, T, N_HEADS, 2 * HEAD_DIM), vs.reshape(S, T, N_HEADS, V_DIM))
```

```python
import functools
import math

import jax
import jax.numpy as jnp
from jax import lax
from jax.experimental import pallas as pl
from jax.experimental.pallas import tpu as pltpu

D_MODEL = 1024
SSM_GROUP = 16
N_GROUPS = D_MODEL // SSM_GROUP
SSM_STATE = 64
N_HEADS = 8
HEAD_DIM = 64
V_DIM = 2 * HEAD_DIM
D_FF = 2816
CONV_W = 3
NORM_EPS = 1e-6
N_A_LAYERS = 1

LANES = 128
LANE_SLABS = D_MODEL // LANES
CHUNK = 16
STATE2 = 2 * SSM_STATE
CHUNK_W = CHUNK * SSM_GROUP
SCAN_STEPS = 7
ACOLS = 8
HIGHEST = lax.Precision.HIGHEST
BF16 = jnp.bfloat16
F32 = jnp.float32
VMEM_LIMIT = 56 * 1024 * 1024


def _cparams(sem, vmem=VMEM_LIMIT):
    return pltpu.CompilerParams(dimension_semantics=sem, vmem_limit_bytes=vmem)


def _resident(shape):
    zeros = (0,) * len(shape)
    return pl.BlockSpec(shape, lambda *_: zeros, pipeline_mode=pl.Buffered(1))


def _inv_rms(x):
    return lax.rsqrt(jnp.mean(x * x, axis=-1, keepdims=True) + NORM_EPS)


def _gelu(x):
    c = math.sqrt(2.0 / math.pi)
    return x * (0.5 * (1.0 + jnp.tanh(c * (x + 0.044715 * (x * x * x)))))


def _dot(a, b):
    return jnp.dot(a, b, preferred_element_type=F32)


def _s5_prep_kernel(lre, lim, ldt, bre, bim, cre, cim, w_ref, z_ref, kmat_ref, acol_ref):
    P = SSM_STATE
    dt = jnp.exp(ldt[...])
    lr, li = lre[...], lim[...]
    ar, ai = lr * dt, li * dt
    first = lax.broadcasted_iota(jnp.int32, (1, 2 * P), 1) < P
    sgn = jnp.where(first, -1.0, 1.0)

    def cpow(n):
        m = jnp.exp(n * ar)
        pr, pi = m * jnp.cos(n * ai), m * jnp.sin(n * ai)
        return jnp.where(first, pr, pi), jnp.where(first, pi, pr)

    def cmul(x1, x2, yr, yi):
        return x1[:, None, :] * yr + (sgn * x2)[:, None, :] * yi

    lbr, lbi = jnp.exp(ar) * jnp.cos(ai), jnp.exp(ar) * jnp.sin(ai)
    den = lr * lr + li * li
    nr, ni = lbr - 1.0, lbi
    cr = (nr * lr + ni * li) / den
    ci = (ni * lr - nr * li) / den
    bbr = cr[:, None, :] * bre[...] - ci[:, None, :] * bim[...]
    bbi = cr[:, None, :] * bim[...] + ci[:, None, :] * bre[...]
    for s in range(CHUNK):
        w_ref[s] = cmul(*cpow(float(CHUNK - 1 - s)), bbr, bbi)
    for i in range(ACOLS):
        n = float(CHUNK * 2 ** i) if i < SCAN_STEPS else float(CHUNK // 2)
        acol_ref[i] = cpow(n)[0]
    for m in range(CHUNK + 1):
        z_ref[m] = -sgn * cmul(*cpow(float(m)), cre[...], cim[...])
    z_all = jnp.concatenate([z_ref[m] for m in range(CHUNK)], axis=1)
    bb = jnp.where(first, bbr, bbi)
    kmat_ref[...] = lax.dot_general(z_all, bb, (((2,), (2,)), ((0,), (0,))), precision=HIGHEST,
                                    preferred_element_type=F32)


def _s5_prepare(lam_re, lam_im, log_dt, b_re, b_im, c_re, c_im):
    G, P, C = N_GROUPS, SSM_STATE, SSM_GROUP
    whole = lambda *shape: pl.BlockSpec(shape, lambda i: (0,) * len(shape))
    dup = lambda a: jnp.concatenate([a, a], axis=-1)
    g2, gc2 = (G, 2 * P), (G, C, 2 * P)
    outs = [(CHUNK,) + gc2, (CHUNK + 1,) + gc2, (G, CHUNK_W, C), (ACOLS,) + g2]
    w, z, kmat, acol = pl.pallas_call(
        _s5_prep_kernel,
        grid=(1,),
        in_specs=[whole(*g2), whole(*g2), whole(G, 1), whole(*gc2), whole(*gc2), whole(*gc2), whole(*gc2)],
        out_specs=[whole(*s) for s in outs],
        out_shape=[jax.ShapeDtypeStruct(s, F32) for s in outs],
        compiler_params=_cparams(("arbitrary",)),
        name="s5_prep",
    )(dup(lam_re), dup(lam_im), log_dt.reshape(G, 1), dup(b_re.transpose(0, 2, 1)),
      dup(b_im.transpose(0, 2, 1)), dup(c_re), dup(c_im))
    wt = w.transpose(1, 3, 0, 2).reshape(G, STATE2, CHUNK_W)
    k4 = kmat.reshape(G, CHUNK, C, C)
    t_idx = jnp.arange(CHUNK)[:, None] - jnp.arange(CHUNK)[None, :]
    toe = jnp.where((t_idx >= 0)[None, :, :, None, None],
                    k4[:, jnp.clip(t_idx, 0, CHUNK - 1)], 0.0)
    toe = toe.transpose(0, 1, 3, 2, 4).reshape(G, CHUNK_W, CHUNK_W)
    m_op = z[1:].transpose(1, 0, 2, 3).reshape(G, CHUNK_W, STATE2)
    tm = jnp.concatenate([toe, m_op], axis=-1)
    return wt.astype(BF16), tm.astype(BF16), acol.transpose(1, 2, 0)


def _cmul(ar, ai, br, bi):
    return ar * br - ai * bi, ar * bi + ai * br


def _s5_mix_prompt_kernel(*refs, nk, gs):
    x_refs = refs[:LANE_SLABS]
    (g_ref, d_ref, wt_ref, tm_ref, acol_ref, z_ref, hout_ref,
     ut_ref, yt_ref, r_ref, carry_ref) = refs[LANE_SLABS:]
    P, D = SSM_STATE, D_MODEL
    nb, gg = pl.program_id(1), pl.program_id(2)

    def slot(s):
        return jnp.concatenate([x[0, pl.ds(s, nk, stride=CHUNK), :] for x in x_refs], axis=1)

    @pl.when(gg == 0)
    def _():
        @pl.when(nb == 0)
        def _():
            carry_ref[...] = jnp.zeros_like(carry_ref)
        for s in range(CHUNK):
            xs = slot(s)
            r = _inv_rms(xs)
            r_ref[s] = r
            ut_ref[s] = (xs * r * g_ref[...]).T.astype(BF16)

    lane = lax.broadcasted_iota(jnp.int32, (P, nk), 1)
    for gi in range(gs):
        g = gg * gs + gi
        row0 = pl.multiple_of(g * SSM_GROUP, SSM_GROUP)
        ug = ut_ref[:, pl.ds(row0, SSM_GROUP), :].reshape(CHUNK_W, nk)
        acol = acol_ref[gi]
        hin = carry_ref[g]
        s_all = _dot(wt_ref[gi], ug)
        cr, ci = _cmul(acol[:P, 0:1], acol[P:, 0:1], hin[:P], hin[P:])
        sr = s_all[:P] + jnp.where(lane == 0, cr, 0.0)
        si = s_all[P:] + jnp.where(lane == 0, ci, 0.0)
        for i in range(SCAN_STEPS):
            sh = 1 << i
            pr = jnp.where(lane >= sh, pltpu.roll(sr, sh, axis=1), 0.0)
            pi = jnp.where(lane >= sh, pltpu.roll(si, sh, axis=1), 0.0)
            qr, qi = _cmul(acol[:P, i:i + 1], acol[P:, i:i + 1], pr, pi)
            sr, si = sr + qr, si + qi
        hpr = jnp.where(lane >= 1, pltpu.roll(sr, 1, axis=1), hin[:P])
        hpi = jnp.where(lane >= 1, pltpu.roll(si, 1, axis=1), hin[P:])
        hend = jnp.concatenate([sr[:, nk - 1:nk], si[:, nk - 1:nk]], axis=0)
        carry_ref[g] = hend
        hout_ref[0, 0, gi] = hend
        hprev = jnp.concatenate([hpr, hpi], axis=0).astype(BF16)
        y = _dot(tm_ref[gi, :, 0:CHUNK_W], ug) + _dot(tm_ref[gi, :, CHUNK_W:], hprev)
        yt_ref[:, pl.ds(row0, SSM_GROUP), :] = y.reshape(CHUNK, SSM_GROUP, nk).astype(BF16)

    @pl.when(gg == pl.num_programs(2) - 1)
    def _():
        for t in range(CHUNK):
            u = slot(t) * r_ref[t] * g_ref[...]
            v = yt_ref[t].astype(F32).T + d_ref[...] * u
            z_ref[0, :, t * D:(t + 1) * D] = _gelu(v).astype(BF16)


def _s5_mix_prompt(x, pre_g, d_skip, wt, tm, acol):
    B, L, D = x.shape
    nk = 1 << SCAN_STEPS
    blk = nk * CHUNK
    assert L % blk == 0
    NB = L // blk
    gs = 8
    G = N_GROUPS
    slab = lambda c: pl.BlockSpec((1, blk, LANES), lambda b, n, g: (b, n, c))
    z2, hout = pl.pallas_call(
        functools.partial(_s5_mix_prompt_kernel, nk=nk, gs=gs),
        grid=(B, NB, G // gs),
        in_specs=[slab(c) for c in range(LANE_SLABS)] + [
                  pl.BlockSpec((1, D), lambda b, n, g: (0, 0)),
                  pl.BlockSpec((1, D), lambda b, n, g: (0, 0)),
                  pl.BlockSpec((gs, STATE2, CHUNK_W), lambda b, n, g: (g, 0, 0)),
                  pl.BlockSpec((gs, CHUNK_W, CHUNK_W + STATE2), lambda b, n, g: (g, 0, 0)),
                  pl.BlockSpec((gs, STATE2, ACOLS), lambda b, n, g: (g, 0, 0))],
        out_specs=[pl.BlockSpec((1, nk, CHUNK * D), lambda b, n, g: (b, n, 0)),
                   pl.BlockSpec((1, 1, gs, STATE2, 1), lambda b, n, g: (b, n, g, 0, 0))],
        out_shape=[jax.ShapeDtypeStruct((B, L // CHUNK, CHUNK * D), BF16),
                   jax.ShapeDtypeStruct((B, NB, G, STATE2, 1), F32)],
        scratch_shapes=[pltpu.VMEM((CHUNK, D, nk), BF16),
                        pltpu.VMEM((CHUNK, D, nk), BF16),
                        pltpu.VMEM((CHUNK, nk, 1), F32),
                        pltpu.VMEM((G, STATE2, 1), F32)],
        compiler_params=_cparams(("arbitrary", "arbitrary", "arbitrary")),
        name="s5_mix_prompt",
    )(*([x] * LANE_SLABS), pre_g.reshape(1, D), d_skip.reshape(1, D), wt, tm, acol)
    return z2, hout[:, NB - 1]


def _s5_mix_sample_kernel(x_ref, g_ref, d_ref, h0_ref, wt_ref, tm_ref, acol_ref, z_ref, hout_ref,
                          ut_ref, yt_ref, r_ref, *, nseq, ntok, gs):
    P, D = SSM_STATE, D_MODEL
    half = ntok * SSM_GROUP
    gg = pl.program_id(0)

    @pl.when(gg == 0)
    def _():
        for s in range(ntok):
            xs = x_ref[:, s * D:(s + 1) * D]
            r = _inv_rms(xs)
            r_ref[s] = r
            ut_ref[s] = (xs * r * g_ref[...]).T.astype(BF16)

    for gi in range(gs):
        g = gg * gs + gi
        row0 = pl.multiple_of(g * SSM_GROUP, SSM_GROUP)
        ug = ut_ref[:, pl.ds(row0, SSM_GROUP), :].reshape(half, nseq)
        h0 = h0_ref[gi]
        acol = acol_ref[gi]
        s_all = _dot(wt_ref[gi, :, half:], ug)
        er, ei = _cmul(acol[:P, ACOLS - 1:ACOLS], acol[P:, ACOLS - 1:ACOLS], h0[:P], h0[P:])
        hout_ref[gi] = jnp.concatenate([er + s_all[:P], ei + s_all[P:]], axis=0)
        y = _dot(tm_ref[gi, 0:half, 0:half], ug) + _dot(tm_ref[gi, 0:half, CHUNK_W:], h0.astype(BF16))
        yt_ref[:, pl.ds(row0, SSM_GROUP), :] = y.reshape(ntok, SSM_GROUP, nseq)

    @pl.when(gg == pl.num_programs(0) - 1)
    def _():
        for t in range(ntok):
            xs = x_ref[:, t * D:(t + 1) * D]
            u = xs * r_ref[t] * g_ref[...]
            v = yt_ref[t].T + d_ref[...] * u
            z_ref[:, t * D:(t + 1) * D] = _gelu(v).astype(BF16)


def _s5_mix_sample(x, h0, pre_g, d_skip, wt, tm, acol):
    S, T, D = x.shape
    assert T * 2 == CHUNK
    gs = 8
    G = N_GROUPS
    z2, hout = pl.pallas_call(
        functools.partial(_s5_mix_sample_kernel, nseq=S, ntok=T, gs=gs),
        grid=(G // gs,),
        in_specs=[pl.BlockSpec((S, T * D), lambda g: (0, 0)),
                  pl.BlockSpec((1, D), lambda g: (0, 0)),
                  pl.BlockSpec((1, D), lambda g: (0, 0)),
                  pl.BlockSpec((gs, STATE2, S), lambda g: (g, 0, 0)),
                  pl.BlockSpec((gs, STATE2, CHUNK_W), lambda g: (g, 0, 0)),
                  pl.BlockSpec((gs, CHUNK_W, CHUNK_W + STATE2), lambda g: (g, 0, 0)),
                  pl.BlockSpec((gs, STATE2, ACOLS), lambda g: (g, 0, 0))],
        out_specs=[pl.BlockSpec((S, T * D), lambda g: (0, 0)),
                   pl.BlockSpec((gs, STATE2, S), lambda g: (g, 0, 0))],
        out_shape=[jax.ShapeDtypeStruct((S, T * D), BF16),
                   jax.ShapeDtypeStruct((G, STATE2, S), F32)],
        scratch_shapes=[pltpu.VMEM((T, D, S), BF16),
                        pltpu.VMEM((T, D, S), F32),
                        pltpu.VMEM((T, S, 1), F32)],
        compiler_params=_cparams(("arbitrary",)),
        name="s5_mix_sample",
    )(x.reshape(S, T * D), pre_g.reshape(1, D), d_skip.reshape(1, D), h0, wt, tm, acol)
    return z2, hout


def _tail_kernel(a_ref, w_ref, g_ref, x_ref, o_ref, *, glu):
    y = _dot(a_ref[...], w_ref[...])
    if glu:
        n = y.shape[-1] // 2
        y = y[:, :n] * jax.nn.sigmoid(y[:, n:])
    o_ref[...] = x_ref[...] + y * _inv_rms(y) * g_ref[...]


def _tail(a2, w, g, x2, *, glu, tm):
    R = a2.shape[0]
    K = w.shape[0]
    D = D_MODEL
    n = a2.shape[1] // K
    tm = min(tm, R)
    assert R % tm == 0 and x2.shape == (R, n * D)
    return pl.pallas_call(
        functools.partial(_tail_kernel, glu=glu),
        grid=(R // tm, n),
        in_specs=[pl.BlockSpec((tm, K), lambda i, s: (i, s)),
                  _resident(w.shape),
                  pl.BlockSpec((1, D), lambda i, s: (0, 0)),
                  pl.BlockSpec((tm, D), lambda i, s: (i, s))],
        out_specs=pl.BlockSpec((tm, D), lambda i, s: (i, s)),
        out_shape=jax.ShapeDtypeStruct((R, n * D), F32),
        compiler_params=_cparams(("parallel", "parallel")),
        name="glu_tail" if glu else "oproj_tail",
    )(a2, w, g.reshape(1, D), x2)


def _glu_tail_prompt_kernel(z_ref, w_ref, g_ref, x_ref, o_ref, slab_ref, *, rk, ns):
    D = D_MODEL
    for s0 in range(0, CHUNK, ns):
        a = jnp.concatenate([z_ref[:, s * D:(s + 1) * D] for s in range(s0, s0 + ns)], axis=0)
        y = _dot(a, w_ref[...])
        y = y[:, :D] * jax.nn.sigmoid(y[:, D:])
        y = y * _inv_rms(y) * g_ref[...]
        for j in range(ns):
            for c in range(LANE_SLABS):
                slab_ref[c, pl.ds(s0 + j, rk, stride=CHUNK), :] = y[j * rk:(j + 1) * rk, c * LANES:(c + 1) * LANES]
    for c in range(LANE_SLABS):
        o_ref[:, c * LANES:(c + 1) * LANES] = x_ref[:, c * LANES:(c + 1) * LANES] + slab_ref[c]


def _glu_tail_prompt(z2, w, g, x2, *, rk, ns):
    R = z2.shape[0]
    D = D_MODEL
    assert R % rk == 0 and CHUNK % ns == 0 and x2.shape == (R * CHUNK, D)
    return pl.pallas_call(
        functools.partial(_glu_tail_prompt_kernel, rk=rk, ns=ns),
        grid=(R // rk,),
        in_specs=[pl.BlockSpec((rk, CHUNK * D), lambda i: (i, 0)),
                  _resident(w.shape),
                  pl.BlockSpec((1, D), lambda i: (0, 0)),
                  pl.BlockSpec((rk * CHUNK, D), lambda i: (i, 0))],
        out_specs=pl.BlockSpec((rk * CHUNK, D), lambda i: (i, 0)),
        out_shape=jax.ShapeDtypeStruct((R * CHUNK, D), F32),
        scratch_shapes=[pltpu.VMEM((LANE_SLABS, rk * CHUNK, LANES), F32)],
        compiler_params=_cparams(("parallel",)),
        name="glu_tail_prompt",
    )(z2, w, g.reshape(1, D), x2)


def _kvq_kernel(x_ref, gkv_ref, gq_ref, wk_ref, wv_ref, wq_ref, k_ref, v_ref, *q_refs, by_head, q_scale):
    x = x_ref[...]
    xr = x * _inv_rms(x)
    kv_in = (xr * gkv_ref[...]).astype(BF16)
    xn = (xr * gq_ref[...]).astype(BF16)
    k = _dot(kv_in, wk_ref[...])
    v = _dot(kv_in, wv_ref[...])
    q = _dot(xn, wq_ref[...]) * q_scale
    if by_head:
        for ref, val in ((k_ref, k), (v_ref, v), (q_refs[0], q)):
            for h in range(N_HEADS):
                ref[:, h, :] = val[:, h * V_DIM:(h + 1) * V_DIM]
    else:
        kb_ref, vb_ref, qb_ref = q_refs
        k_ref[...] = k
        v_ref[...] = v
        kb_ref[...] = k.astype(BF16)
        vb_ref[...] = v.astype(BF16)
        qb_ref[...] = q.astype(BF16)


def _kvq(x2, g_kv, g_q, wk, wv, wq, *, tm, by_head, q_scale):
    R, D = x2.shape
    tm = min(tm, R)
    assert R % tm == 0
    tile = pl.BlockSpec((tm, D), lambda i: (i, 0))
    heads = pl.BlockSpec((tm, N_HEADS, V_DIM), lambda i: (i, 0, 0))
    vec = pl.BlockSpec((1, D), lambda i: (0, 0))
    head32 = jax.ShapeDtypeStruct((R, N_HEADS, V_DIM), F32)
    flat32 = jax.ShapeDtypeStruct((R, D), F32)
    flat16 = jax.ShapeDtypeStruct((R, D), BF16)
    return pl.pallas_call(
        functools.partial(_kvq_kernel, by_head=by_head, q_scale=q_scale),
        grid=(R // tm,),
        in_specs=[tile, vec, vec, _resident(wk.shape), _resident(wv.shape), _resident(wq.shape)],
        out_specs=[heads] * 3 if by_head else [tile] * 5,
        out_shape=[head32] * 3 if by_head else [flat32, flat32, flat16, flat16, flat16],
        compiler_params=_cparams(("parallel",)),
        name="kvq_proj",
    )(x2, g_kv.reshape(1, D), g_q.reshape(1, D), wk, wv, wq)


FF_CHUNK = D_FF // 2


def _ffn_cols(c):
    return (slice(c * FF_CHUNK, (c + 1) * FF_CHUNK),
            slice(D_FF + c * FF_CHUNK, D_FF + (c + 1) * FF_CHUNK))


def _ffn_prompt_kernel(x_ref, gpre_ref, gpost_ref, wup_ref, cw_ref, cb_ref, wdn_ref,
                       o_ref, cs_ref, carry_ref, *, tm):
    @pl.when(pl.program_id(1) == 0)
    def _():
        carry_ref[...] = jnp.zeros_like(carry_ref)

    x = x_ref[0]
    xn = (x * _inv_rms(x) * gpre_ref[...]).astype(BF16)
    rows = lax.broadcasted_iota(jnp.int32, (tm, 1), 0)
    f = jnp.zeros((tm, D_MODEL), F32)
    for c in range(D_FF // FF_CHUNK):
        halves = []
        for cols in _ffn_cols(c):
            h = _dot(xn, wup_ref[:, cols])
            c0, c1 = carry_ref[6:7, cols], carry_ref[7:8, cols]
            h1 = jnp.where(rows == 0, c1, pltpu.roll(h, 1, axis=0))
            h2 = jnp.where(rows == 0, c0, jnp.where(rows == 1, c1, pltpu.roll(h, 2, axis=0)))
            halves.append(cb_ref[:, cols] + cw_ref[2:3, cols] * h
                          + cw_ref[1:2, cols] * h1 + cw_ref[0:1, cols] * h2)
            carry_ref[:, cols] = h[tm - 8:tm]
            cs_ref[0, :, cols] = h[tm - 2:tm]
        a = (_gelu(halves[0]) * halves[1]).astype(BF16)
        f = f + _dot(a, wdn_ref[c * FF_CHUNK:(c + 1) * FF_CHUNK, :])
    o_ref[0] = x + f * _inv_rms(f) * gpost_ref[...]


def _ffn_prompt(x, gpre, gpost, wup, cw, cb, wdn, *, tm):
    B, L, D = x.shape
    tm = min(tm, L)
    assert L % tm == 0
    F2 = 2 * D_FF
    vec = pl.BlockSpec((1, D), lambda b, t: (0, 0))
    return pl.pallas_call(
        functools.partial(_ffn_prompt_kernel, tm=tm),
        grid=(B, L // tm),
        in_specs=[pl.BlockSpec((1, tm, D), lambda b, t: (b, t, 0)), vec, vec,
                  _resident(wup.shape), _resident((CONV_W, F2)), _resident((1, F2)),
                  _resident(wdn.shape)],
        out_specs=[pl.BlockSpec((1, tm, D), lambda b, t: (b, t, 0)),
                   pl.BlockSpec((1, CONV_W - 1, F2), lambda b, t: (b, 0, 0))],
        out_shape=[jax.ShapeDtypeStruct((B, L, D), F32),
                   jax.ShapeDtypeStruct((B, CONV_W - 1, F2), F32)],
        scratch_shapes=[pltpu.VMEM((8, F2), F32)],
        compiler_params=_cparams(("arbitrary", "arbitrary")),
        name="ffn_prompt",
    )(x, gpre.reshape(1, D), gpost.reshape(1, D), wup, cw, cb.reshape(1, F2), wdn)


def _ffn_sample_kernel(x_ref, buf_ref, gpre_ref, gpost_ref, wup_ref, cw_ref, cb_ref, wdn_ref,
                       o_ref, cs_ref, *, tm, ntok):
    nsq = tm // ntok
    x = x_ref[...]
    xn = (x * _inv_rms(x) * gpre_ref[...]).astype(BF16)
    sub = lax.broadcasted_iota(jnp.int32, (tm, 1), 0) % ntok
    f = jnp.zeros((tm, D_MODEL), F32)
    for c in range(D_FF // FF_CHUNK):
        halves = []
        for cols in _ffn_cols(c):
            h = _dot(xn, wup_ref[:, cols])
            spread = lambda b: jnp.broadcast_to(b, (nsq, ntok, FF_CHUNK)).reshape(tm, FF_CHUNK)
            b0, b1 = spread(buf_ref[:, 0:1, cols]), spread(buf_ref[:, 1:2, cols])
            h1 = jnp.where(sub == 0, b1, pltpu.roll(h, 1, axis=0))
            h2 = jnp.where(sub == 0, b0, jnp.where(sub == 1, b1, pltpu.roll(h, 2, axis=0)))
            halves.append(cb_ref[:, cols] + cw_ref[2:3, cols] * h
                          + cw_ref[1:2, cols] * h1 + cw_ref[0:1, cols] * h2)
            cs_ref[:, :, cols] = h.reshape(nsq, ntok, FF_CHUNK)[:, ntok - 2:ntok, :]
        a = (_gelu(halves[0]) * halves[1]).astype(BF16)
        f = f + _dot(a, wdn_ref[c * FF_CHUNK:(c + 1) * FF_CHUNK, :])
    o_ref[...] = x + f * _inv_rms(f) * gpost_ref[...]


def _ffn_sample(x, buf, gpre, gpost, wup, cw, cb, wdn, *, tm):
    S, T, D = x.shape
    assert T == 8 and tm % T == 0 and (S * T) % tm == 0
    F2 = 2 * D_FF
    nsq = tm // T
    vec = pl.BlockSpec((1, D), lambda i: (0, 0))
    out, cs = pl.pallas_call(
        functools.partial(_ffn_sample_kernel, tm=tm, ntok=T),
        grid=(S * T // tm,),
        in_specs=[pl.BlockSpec((tm, D), lambda i: (i, 0)),
                  pl.BlockSpec((nsq, CONV_W - 1, F2), lambda i: (i, 0, 0)), vec, vec,
                  _resident(wup.shape), _resident((CONV_W, F2)), _resident((1, F2)),
                  _resident(wdn.shape)],
        out_specs=[pl.BlockSpec((tm, D), lambda i: (i, 0)),
                   pl.BlockSpec((nsq, CONV_W - 1, F2), lambda i: (i, 0, 0))],
        out_shape=[jax.ShapeDtypeStruct((S * T, D), F32),
                   jax.ShapeDtypeStruct((S, CONV_W - 1, F2), F32)],
        compiler_params=_cparams(("parallel",)),
        name="ffn_sample",
    )(x.reshape(S * T, D), buf, gpre.reshape(1, D), gpost.reshape(1, D), wup, cw,
      cb.reshape(1, F2), wdn)
    return out.reshape(S, T, D), cs


def _lam_value(lq1, lk1, lq2, lk2, lam_init):
    return (jnp.exp(jnp.sum(lq1[...] * lk1[...], axis=-1, keepdims=True))
            - jnp.exp(jnp.sum(lq2[...] * lk2[...], axis=-1, keepdims=True)) + lam_init)


def _sub_norm(o, subg, lam_init):
    return o * _inv_rms(o) * subg * (1.0 - lam_init)


LOG2E = math.log2(math.e)
FLASH_ROWS = 256
POS_SPLIT = 32


def _flash_kernel(qi_ref, ki_ref, q_ref, k_ref, v_ref, lq1, lk1, lq2, lk2, subg_ref, o_ref,
                  q2_ref, kf_ref, m_ref, acc_ref, *, tq, lam_init):
    h, step = pl.program_id(1), pl.program_id(2)
    qi, ki = qi_ref[step], ki_ref[step]
    RQ = FLASH_ROWS
    n_chunks = tq // RQ
    lane = lax.broadcasted_iota(jnp.int32, (1, V_DIM), 1)
    slope2 = jnp.exp2(-jnp.full((1, V_DIM), h + 1, jnp.int32).astype(F32)) * LOG2E

    @pl.when(ki == 0)
    def _():
        m_ref[...] = jnp.full_like(m_ref, -jnp.inf)
        acc_ref[...] = jnp.zeros_like(acc_ref)
        kidx = lax.broadcasted_iota(jnp.int32, (tq, 1), 0)
        a, b = (kidx // POS_SPLIT).astype(F32), (kidx % POS_SPLIT).astype(F32)
        kf_ref[...] = jnp.where(lane < 3, a, jnp.where(lane < 6, b, 0.0)).astype(BF16)
        c0 = slope2.astype(BF16).astype(F32)
        c1 = (slope2 - c0).astype(BF16).astype(F32)
        c2 = (slope2 - c0 - c1).astype(BF16).astype(F32)
        part = jnp.where(lane % 3 == 0, c0, jnp.where(lane % 3 == 1, c1, c2))
        qf = jnp.where(lane < 3, POS_SPLIT * part, jnp.where(lane < 6, part, 0.0)).astype(BF16)
        qf = jnp.broadcast_to(qf, (RQ, V_DIM))
        for c_i in range(n_chunks):
            q = q_ref[c_i * RQ:(c_i + 1) * RQ, :]
            zero = jnp.zeros_like(q)
            q2_ref[c_i, 0:RQ, 0:V_DIM] = jnp.where(lane < HEAD_DIM, q, zero)
            q2_ref[c_i, RQ:2 * RQ, 0:V_DIM] = jnp.where(lane >= HEAD_DIM, q, zero)
            q2_ref[c_i, 0:RQ, V_DIM:] = qf
            q2_ref[c_i, RQ:2 * RQ, V_DIM:] = qf

    def absorb(diagonal):
        k_aug = jnp.concatenate([k_ref[...], kf_ref[...]], axis=1)
        v_aug = jnp.concatenate([v_ref[...], jnp.ones((tq, V_DIM), BF16)], axis=1)
        shift = slope2 * ((ki - qi) * tq).astype(F32)
        for c_i in range(n_chunks):
            nkeys = (c_i + 1) * RQ if diagonal else tq
            s = lax.dot_general(q2_ref[c_i], k_aug[0:nkeys], (((1,), (1,)), ((), ())),
                                preferred_element_type=F32)
            if diagonal:
                kcol = lax.broadcasted_iota(jnp.int32, (1, RQ), 1)
                qrow = lax.broadcasted_iota(jnp.int32, (2 * RQ, 1), 0) % RQ
                tail = jnp.where(kcol <= qrow, s[:, nkeys - RQ:], -jnp.inf)
                s = tail if nkeys == RQ else jnp.concatenate([s[:, 0:nkeys - RQ], tail], axis=1)
            m_old = m_ref[c_i]
            m_new = jnp.maximum(m_old, jnp.max(s, axis=-1, keepdims=True) + shift)
            alpha = jnp.exp2(m_old - m_new)
            p = jnp.exp2(s - jnp.concatenate([m_new - shift] * (nkeys // V_DIM), axis=1))
            pv = _dot(p.astype(BF16), v_aug[0:nkeys])
            acc_ref[c_i] = jnp.concatenate([alpha, alpha], axis=1) * acc_ref[c_i] + pv
            m_ref[c_i] = m_new

    @pl.when(ki < qi)
    def _():
        absorb(False)

    @pl.when(ki == qi)
    def _():
        absorb(True)
        lam = _lam_value(lq1, lk1, lq2, lk2, lam_init)
        for c_i in range(n_chunks):
            acc = acc_ref[c_i]
            o = (acc[0:RQ, 0:V_DIM] / acc[0:RQ, V_DIM:]
                 - lam * (acc[RQ:, 0:V_DIM] / acc[RQ:, V_DIM:]))
            o_ref[c_i * RQ:(c_i + 1) * RQ, :] = _sub_norm(o, subg_ref[...], lam_init).astype(o_ref.dtype)


def _flash_prompt(qb, kb, vb, lq1, lk1, lq2, lk2, subg, *, B, L, tq, lam_init):
    tq = min(tq, L)
    assert L % tq == 0 and tq % FLASH_ROWS == 0 and tq <= POS_SPLIT * POS_SPLIT
    nq = L // tq
    n_chunks = tq // FLASH_ROWS
    pairs = [(i, j) for i in range(nq) for j in range(i + 1)]
    qi_tab = jnp.asarray([p[0] for p in pairs], jnp.int32)
    ki_tab = jnp.asarray([p[1] for p in pairs], jnp.int32)
    lvec = pl.BlockSpec((1, HEAD_DIM), lambda b, h, s, qt, kt: (0, 0))
    grid_spec = pltpu.PrefetchScalarGridSpec(
        num_scalar_prefetch=2,
        grid=(B, N_HEADS, len(pairs)),
        in_specs=[pl.BlockSpec((tq, V_DIM), lambda b, h, s, qt, kt: (b * nq + qt[s], h)),
                  pl.BlockSpec((tq, V_DIM), lambda b, h, s, qt, kt: (b * nq + kt[s], h)),
                  pl.BlockSpec((tq, V_DIM), lambda b, h, s, qt, kt: (b * nq + kt[s], h)),
                  lvec, lvec, lvec, lvec,
                  pl.BlockSpec((1, V_DIM), lambda b, h, s, qt, kt: (0, 0))],
        out_specs=pl.BlockSpec((tq, V_DIM), lambda b, h, s, qt, kt: (b * nq + qt[s], h)),
        scratch_shapes=[pltpu.VMEM((n_chunks, 2 * FLASH_ROWS, 2 * V_DIM), BF16),
                        pltpu.VMEM((tq, V_DIM), BF16),
                        pltpu.VMEM((n_chunks, 2 * FLASH_ROWS, V_DIM), F32),
                        pltpu.VMEM((n_chunks, 2 * FLASH_ROWS, 2 * V_DIM), F32)])
    return pl.pallas_call(
        functools.partial(_flash_kernel, tq=tq, lam_init=lam_init),
        grid_spec=grid_spec,
        out_shape=jax.ShapeDtypeStruct((B * L, N_HEADS * V_DIM), BF16),
        compiler_params=_cparams(("parallel", "parallel", "arbitrary")),
        name="flash_prompt",
    )(qi_tab, ki_tab, qb, kb, vb, lq1, lk1, lq2, lk2, subg)


def _paged_kernel(pt_ref, q_ref, *refs, ntok, page, npp, past_len, lam_init):
    kc = refs[:npp]
    vc = refs[npp:2 * npp]
    (kn_ref, vn_ref, lq1, lk1, lq2, lk2, subg_ref, o_ref,
     qall_ref, bias_ref, m_ref, l_ref, acc_ref) = refs[2 * npp:]
    H = N_HEADS
    R = 2 * ntok * H
    PW = page * H
    pg = pl.program_id(1)

    row = lax.broadcasted_iota(jnp.int32, (R, 1), 0)
    slope_r = jnp.exp2(-(row % H + 1).astype(F32))
    qi_r = (row // H) % ntok

    def head_bias(n_lanes, key_limit):
        lane = lax.broadcasted_iota(jnp.int32, (1, n_lanes), 1)
        key = lane // H
        ok = (lane % H == row % H) & (key < key_limit)
        return jnp.where(ok, slope_r * (key - qi_r).astype(F32), -jnp.inf)

    @pl.when(pg == 0)
    def _():
        m_ref[...] = jnp.full_like(m_ref, -jnp.inf)
        l_ref[...] = jnp.zeros_like(l_ref)
        acc_ref[...] = jnp.zeros_like(acc_ref)
        q2 = q_ref[...].reshape(ntok * H, V_DIM)
        col = lax.broadcasted_iota(jnp.int32, (1, V_DIM), 1)
        qall_ref[...] = jnp.concatenate([jnp.where(col < HEAD_DIM, q2, 0.0),
                                         jnp.where(col >= HEAD_DIM, q2, 0.0)], axis=0).astype(BF16)
        bias_ref[...] = head_bias(PW, page)

    def absorb(ks, vs, biases):
        qall = qall_ref[...]
        s = [lax.dot_general(qall, k, (((1,), (1,)), ((), ())), preferred_element_type=F32) + b
             for k, b in zip(ks, biases)]
        m_old = m_ref[...]
        m_new = m_old
        for si in s:
            m_new = jnp.maximum(m_new, jnp.max(si, axis=-1, keepdims=True))
        alpha = jnp.exp(m_old - m_new)
        l_new = alpha * l_ref[...]
        acc = alpha * acc_ref[...]
        for si, v in zip(s, vs):
            p = jnp.exp(si - m_new)
            l_new = l_new + jnp.sum(p, axis=-1, keepdims=True)
            acc = acc + _dot(p.astype(BF16), v)
        m_ref[...] = m_new
        l_ref[...] = l_new
        acc_ref[...] = acc

    flat = lambda ref: ref[0].reshape(PW, V_DIM).astype(BF16)
    biases = [bias_ref[...] + slope_r * ((pg * npp + i) * page - past_len).astype(F32) for i in range(npp)]
    absorb([flat(r) for r in kc], [flat(r) for r in vc], biases)

    @pl.when(pg == pl.num_programs(1) - 1)
    def _():
        n_new = ntok * H
        pad = jnp.zeros((R - n_new, V_DIM), F32)
        kn = jnp.concatenate([kn_ref[...].reshape(n_new, V_DIM), pad], axis=0).astype(BF16)
        vn = jnp.concatenate([vn_ref[...].reshape(n_new, V_DIM), pad], axis=0).astype(BF16)
        absorb([kn], [vn], [head_bias(R, jnp.minimum(qi_r + 1, ntok))])
        lam = _lam_value(lq1, lk1, lq2, lk2, lam_init)
        h = R // 2
        o = acc_ref[0:h] / l_ref[0:h] - lam * (acc_ref[h:R] / l_ref[h:R])
        o_ref[...] = _sub_norm(o, subg_ref[...], lam_init).reshape(ntok, H, V_DIM)


def _paged_sample(page_table, q, cache_k, cache_v, k_new, v_new, lq1, lk1, lq2, lk2, subg,
                  *, ntok, npp, lam_init):
    S, n_pages = page_table.shape
    page = cache_k.shape[1]
    H = N_HEADS
    R = 2 * ntok * H
    assert n_pages % npp == 0
    tok = pl.BlockSpec((ntok, H, V_DIM), lambda s, p, pt: (s, 0, 0))
    pgs = [pl.BlockSpec((1, page, H, V_DIM), functools.partial(
        lambda s, p, pt, i: (pt[s, p * npp + i], 0, 0, 0), i=i)) for i in range(npp)]
    lvec = pl.BlockSpec((1, HEAD_DIM), lambda s, p, pt: (0, 0))
    grid_spec = pltpu.PrefetchScalarGridSpec(
        num_scalar_prefetch=1,
        grid=(S, n_pages // npp),
        in_specs=[tok] + pgs + pgs + [tok, tok, lvec, lvec, lvec, lvec,
                                      pl.BlockSpec((1, V_DIM), lambda s, p, pt: (0, 0))],
        out_specs=tok,
        scratch_shapes=[pltpu.VMEM((R, V_DIM), BF16), pltpu.VMEM((R, page * H), F32),
                        pltpu.VMEM((R, 1), F32), pltpu.VMEM((R, 1), F32), pltpu.VMEM((R, V_DIM), F32)])
    return pl.pallas_call(
        functools.partial(_paged_kernel, ntok=ntok, page=page, npp=npp, past_len=n_pages * page,
                          lam_init=lam_init),
        grid_spec=grid_spec,
        out_shape=jax.ShapeDtypeStruct((S * ntok, H, V_DIM), F32),
        compiler_params=_cparams(("arbitrary", "arbitrary")),
        name="paged_sample",
    )(page_table, q, *([cache_k] * npp), *([cache_v] * npp), k_new, v_new, lq1, lk1, lq2, lk2, subg)


def kernel(x_prompt, x_sample, state_ssm_re, state_ssm_im, state_conv, cache_k, cache_v, page_table,
           a_pre_g, a_post_g, ssm_lam_re, ssm_lam_im, ssm_log_dt, ssm_b_re, ssm_b_im, ssm_c_re,
           ssm_c_im, ssm_d, glu_w, kv_norm_g, w_k, w_v, b_pre_g, b_post_g, w_q, lam_q1, lam_k1,
           lam_q2, lam_k2, sub_g, w_o, f_pre_g, f_post_g, w_up, conv_w, conv_b, w_down):
    B, L, D = x_prompt.shape
    S, T, _ = x_sample.shape
    P, G = SSM_STATE, N_GROUPS
    HW = N_HEADS * V_DIM
    st_dtype = state_ssm_re.dtype

    wt, tm_op, acol = _s5_prepare(ssm_lam_re[0], ssm_lam_im[0], ssm_log_dt[0], ssm_b_re[0],
                                  ssm_b_im[0], ssm_c_re[0], ssm_c_im[0])
    glu_b = glu_w[0].astype(BF16)
    zp, hp = _s5_mix_prompt(x_prompt, a_pre_g[0], ssm_d[0], wt, tm_op, acol)
    xp = _glu_tail_prompt(zp.reshape(B * L // CHUNK, CHUNK * D), glu_b, a_post_g[0],
                          x_prompt.reshape(B * L, D), rk=64, ns=4).reshape(B, L, D)
    h0 = jnp.concatenate([state_ssm_re[0].astype(F32), state_ssm_im[0].astype(F32)], axis=-1)
    zs, hs = _s5_mix_sample(x_sample, h0.transpose(1, 2, 0), a_pre_g[0], ssm_d[0], wt, tm_op, acol)
    xs = _tail(zs, glu_b, a_post_g[0], x_sample.reshape(S, T * D), glu=True, tm=512).reshape(S, T, D)

    hp = hp.reshape(B, G, 2 * P)
    hs = hs.transpose(2, 0, 1)
    ssm_re_p, ssm_im_p = hp[None, ..., :P].astype(st_dtype), hp[None, ..., P:].astype(st_dtype)
    ssm_re_s, ssm_im_s = hs[None, ..., :P].astype(st_dtype), hs[None, ..., P:].astype(st_dtype)

    wup_b, wdn_b = w_up.astype(BF16), w_down.astype(BF16)
    xp, conv_p0 = _ffn_prompt(xp, f_pre_g[0], f_post_g[0], wup_b[0], conv_w[0], conv_b[0], wdn_b[0], tm=256)
    xs, conv_s0 = _ffn_sample(xs, state_conv[0], f_pre_g[0], f_post_g[0], wup_b[0], conv_w[0],
                              conv_b[0], wdn_b[0], tm=256)

    lam_init = 0.8 - 0.6 * math.exp(-0.3 * N_A_LAYERS)
    wk_b, wv_b, wq_b, wo_b = w_k.astype(BF16), w_v.astype(BF16), w_q[0].astype(BF16), w_o[0].astype(BF16)
    lvec = lambda a: a[0].reshape(1, HEAD_DIM).astype(F32)
    lams = (lvec(lam_q1), lvec(lam_k1), lvec(lam_q2), lvec(lam_k2))
    subg = sub_g[0].reshape(1, V_DIM)

    xp2 = xp.reshape(B * L, D)
    kp, vp, kpb, vpb, qpb = _kvq(xp2, kv_norm_g, b_pre_g[0], wk_b, wv_b, wq_b, tm=512, by_head=False,
                                 q_scale=LOG2E * HEAD_DIM ** -0.5)
    op = _flash_prompt(qpb, kpb, vpb, *lams, subg, B=B, L=L, tq=1024, lam_init=lam_init)
    xp = _tail(op, wo_b, b_post_g[0], xp2, glu=False, tm=512).reshape(B, L, D)

    xs2 = xs.reshape(S * T, D)
    ks, vs, qs = _kvq(xs2, kv_norm_g, b_pre_g[0], wk_b, wv_b, wq_b, tm=512, by_head=True,
                      q_scale=HEAD_DIM ** -0.5)
    os_ = _paged_sample(page_table, qs, cache_k, cache_v, ks, vs, *lams, subg, ntok=T, npp=8,
                        lam_init=lam_init)
    xs = _tail(os_.reshape(S * T, HW).astype(BF16), wo_b, b_post_g[0], xs2, glu=False, tm=512).reshape(S, T, D)

    xp, conv_p1 = _ffn_prompt(xp, f_pre_g[1], f_post_g[1], wup_b[1], conv_w[1], conv_b[1], wdn_b[1], tm=256)
    xs, conv_s1 = _ffn_sample(xs, state_conv[1], f_pre_g[1], f_post_g[1], wup_b[1], conv_w[1],
                              conv_b[1], wdn_b[1], tm=256)

    return (xp, xs, ssm_re_p, ssm_im_p, jnp.stack([conv_p0, conv_p1]),
            kp.reshape(B, L, N_HEADS, 2 * HEAD_DIM), vp.reshape(B, L, N_HEADS, V_DIM),
            ssm_re_s, ssm_im_s, jnp.stack([conv_s0, conv_s1]),
            ks.reshape(S, T, N_HEADS, 2 * HEAD_DIM), vs.reshape(S, T, N_HEADS, V_DIM))
```

```python
import functools
import math

import jax
import jax.numpy as jnp
from jax import lax
from jax.experimental import pallas as pl
from jax.experimental.pallas import tpu as pltpu

D_MODEL = 1024
SSM_GROUP = 16
N_GROUPS = D_MODEL // SSM_GROUP
SSM_STATE = 64
N_HEADS = 8
HEAD_DIM = 64
V_DIM = 2 * HEAD_DIM
D_FF = 2816
CONV_W = 3
NORM_EPS = 1e-6
N_A_LAYERS = 1

LANES = 128
LANE_SLABS = D_MODEL // LANES
CHUNK = 16
STATE2 = 2 * SSM_STATE
CHUNK_W = CHUNK * SSM_GROUP
SCAN_STEPS = 7
ACOLS = 8
HIGHEST = lax.Precision.HIGHEST
BF16 = jnp.bfloat16
F32 = jnp.float32
VMEM_LIMIT = 56 * 1024 * 1024


def _cparams(sem, vmem=VMEM_LIMIT):
    return pltpu.CompilerParams(dimension_semantics=sem, vmem_limit_bytes=vmem)


def _resident(shape):
    zeros = (0,) * len(shape)
    return pl.BlockSpec(shape, lambda *_: zeros, pipeline_mode=pl.Buffered(1))


def _inv_rms(x):
    return lax.rsqrt(jnp.mean(x * x, axis=-1, keepdims=True) + NORM_EPS)


def _gelu(x):
    c = math.sqrt(2.0 / math.pi)
    return x * (0.5 * (1.0 + jnp.tanh(c * (x + 0.044715 * (x * x * x)))))


def _dot(a, b):
    return jnp.dot(a, b, preferred_element_type=F32)


def _s5_prep_kernel(lre, lim, ldt, bre, bim, cre, cim, w_ref, z_ref, kmat_ref, acol_ref):
    P = SSM_STATE
    dt = jnp.exp(ldt[...])
    lr, li = lre[...], lim[...]
    ar, ai = lr * dt, li * dt
    first = lax.broadcasted_iota(jnp.int32, (1, 2 * P), 1) < P
    sgn = jnp.where(first, -1.0, 1.0)

    def cpow(n):
        m = jnp.exp(n * ar)
        pr, pi = m * jnp.cos(n * ai), m * jnp.sin(n * ai)
        return jnp.where(first, pr, pi), jnp.where(first, pi, pr)

    def cmul(x1, x2, yr, yi):
        return x1[:, None, :] * yr + (sgn * x2)[:, None, :] * yi

    lbr, lbi = jnp.exp(ar) * jnp.cos(ai), jnp.exp(ar) * jnp.sin(ai)
    den = lr * lr + li * li
    nr, ni = lbr - 1.0, lbi
    cr = (nr * lr + ni * li) / den
    ci = (ni * lr - nr * li) / den
    bbr = cr[:, None, :] * bre[...] - ci[:, None, :] * bim[...]
    bbi = cr[:, None, :] * bim[...] + ci[:, None, :] * bre[...]
    for s in range(CHUNK):
        w_ref[s] = cmul(*cpow(float(CHUNK - 1 - s)), bbr, bbi)
    for i in range(ACOLS):
        n = float(CHUNK * 2 ** i) if i < SCAN_STEPS else float(CHUNK // 2)
        acol_ref[i] = cpow(n)[0]
    for m in range(CHUNK + 1):
        z_ref[m] = -sgn * cmul(*cpow(float(m)), cre[...], cim[...])
    z_all = jnp.concatenate([z_ref[m] for m in range(CHUNK)], axis=1)
    bb = jnp.where(first, bbr, bbi)
    kmat_ref[...] = lax.dot_general(z_all, bb, (((2,), (2,)), ((0,), (0,))), precision=HIGHEST,
                                    preferred_element_type=F32)


def _s5_prepare(lam_re, lam_im, log_dt, b_re, b_im, c_re, c_im):
    G, P, C = N_GROUPS, SSM_STATE, SSM_GROUP
    whole = lambda *shape: pl.BlockSpec(shape, lambda i: (0,) * len(shape))
    dup = lambda a: jnp.concatenate([a, a], axis=-1)
    g2, gc2 = (G, 2 * P), (G, C, 2 * P)
    outs = [(CHUNK,) + gc2, (CHUNK + 1,) + gc2, (G, CHUNK_W, C), (ACOLS,) + g2]
    w, z, kmat, acol = pl.pallas_call(
        _s5_prep_kernel,
        grid=(1,),
        in_specs=[whole(*g2), whole(*g2), whole(G, 1), whole(*gc2), whole(*gc2), whole(*gc2), whole(*gc2)],
        out_specs=[whole(*s) for s in outs],
        out_shape=[jax.ShapeDtypeStruct(s, F32) for s in outs],
        compiler_params=_cparams(("arbitrary",)),
        name="s5_prep",
    )(dup(lam_re), dup(lam_im), log_dt.reshape(G, 1), dup(b_re.transpose(0, 2, 1)),
      dup(b_im.transpose(0, 2, 1)), dup(c_re), dup(c_im))
    wt = w.transpose(1, 3, 0, 2).reshape(G, STATE2, CHUNK_W)
    k4 = kmat.reshape(G, CHUNK, C, C)
    t_idx = jnp.arange(CHUNK)[:, None] - jnp.arange(CHUNK)[None, :]
    toe = jnp.where((t_idx >= 0)[None, :, :, None, None],
                    k4[:, jnp.clip(t_idx, 0, CHUNK - 1)], 0.0)
    toe = toe.transpose(0, 1, 3, 2, 4).reshape(G, CHUNK_W, CHUNK_W)
    m_op = z[1:].transpose(1, 0, 2, 3).reshape(G, CHUNK_W, STATE2)
    tm = jnp.concatenate([toe, m_op], axis=-1)
    return wt.astype(BF16), tm.astype(BF16), acol.transpose(1, 2, 0)


def _cmul(ar, ai, br, bi):
    return ar * br - ai * bi, ar * bi + ai * br


def _s5_mix_prompt_kernel(*refs, nk, gs):
    x_refs = refs[:LANE_SLABS]
    (g_ref, d_ref, wt_ref, tm_ref, acol_ref, z_ref, hout_ref,
     ut_ref, yt_ref, r_ref, carry_ref) = refs[LANE_SLABS:]
    P, D = SSM_STATE, D_MODEL
    nb, gg = pl.program_id(1), pl.program_id(2)

    def slot(s):
        return jnp.concatenate([x[0, pl.ds(s, nk, stride=CHUNK), :] for x in x_refs], axis=1)

    @pl.when(gg == 0)
    def _():
        @pl.when(nb == 0)
        def _():
            carry_ref[...] = jnp.zeros_like(carry_ref)
        for s in range(CHUNK):
            xs = slot(s)
            r = _inv_rms(xs)
            r_ref[s] = r
            ut_ref[s] = (xs * r * g_ref[...]).T.astype(BF16)

    lane = lax.broadcasted_iota(jnp.int32, (gs * P, nk), 1)
    row0 = pl.multiple_of(gg * (gs * SSM_GROUP), gs * SSM_GROUP)
    u_all = ut_ref[:, pl.ds(row0, gs * SSM_GROUP), :]
    ugs = [u_all[:, gi * SSM_GROUP:(gi + 1) * SSM_GROUP, :].reshape(CHUNK_W, nk) for gi in range(gs)]
    s_loc = [_dot(wt_ref[gi], ugs[gi]) for gi in range(gs)]
    stack = lambda parts: jnp.concatenate(parts, axis=0)
    hin = carry_ref[pl.ds(gg * gs, gs)]
    hin_r, hin_i = hin[:, :P].reshape(gs * P, 1), hin[:, P:].reshape(gs * P, 1)
    acol = acol_ref[...]
    a_r, a_i = acol[:, :P].reshape(gs * P, ACOLS), acol[:, P:].reshape(gs * P, ACOLS)
    cr, ci = _cmul(a_r[:, 0:1], a_i[:, 0:1], hin_r, hin_i)
    sr = stack([s[:P] for s in s_loc]) + jnp.where(lane == 0, cr, 0.0)
    si = stack([s[P:] for s in s_loc]) + jnp.where(lane == 0, ci, 0.0)
    for i in range(SCAN_STEPS):
        sh = 1 << i
        pr = jnp.where(lane >= sh, pltpu.roll(sr, sh, axis=1), 0.0)
        pi = jnp.where(lane >= sh, pltpu.roll(si, sh, axis=1), 0.0)
        qr, qi = _cmul(a_r[:, i:i + 1], a_i[:, i:i + 1], pr, pi)
        sr, si = sr + qr, si + qi
    hpr = jnp.where(lane >= 1, pltpu.roll(sr, 1, axis=1), hin_r)
    hpi = jnp.where(lane >= 1, pltpu.roll(si, 1, axis=1), hin_i)
    end_r, end_i = sr[:, nk - 1:nk].reshape(gs, P, 1), si[:, nk - 1:nk].reshape(gs, P, 1)
    hend = jnp.concatenate([end_r, end_i], axis=1)
    carry_ref[pl.ds(gg * gs, gs)] = hend
    hout_ref[0, 0] = hend
    ys = []
    for gi in range(gs):
        rows = slice(gi * P, (gi + 1) * P)
        hprev = stack([hpr[rows], hpi[rows]]).astype(BF16)
        y = _dot(tm_ref[gi, :, 0:CHUNK_W], ugs[gi]) + _dot(tm_ref[gi, :, CHUNK_W:], hprev)
        ys.append(y.reshape(CHUNK, SSM_GROUP, nk).astype(BF16))
    yt_ref[:, pl.ds(row0, gs * SSM_GROUP), :] = jnp.concatenate(ys, axis=1)

    @pl.when(gg == pl.num_programs(2) - 1)
    def _():
        for t in range(CHUNK):
            u = slot(t) * r_ref[t] * g_ref[...]
            v = yt_ref[t].astype(F32).T + d_ref[...] * u
            z_ref[0, :, t * D:(t + 1) * D] = _gelu(v).astype(BF16)


def _s5_mix_prompt(x, pre_g, d_skip, wt, tm, acol):
    B, L, D = x.shape
    nk = 1 << SCAN_STEPS
    blk = nk * CHUNK
    assert L % blk == 0
    NB = L // blk
    gs = 8
    G = N_GROUPS
    slab = lambda c: pl.BlockSpec((1, blk, LANES), lambda b, n, g: (b, n, c))
    z2, hout = pl.pallas_call(
        functools.partial(_s5_mix_prompt_kernel, nk=nk, gs=gs),
        grid=(B, NB, G // gs),
        in_specs=[slab(c) for c in range(LANE_SLABS)] + [
                  pl.BlockSpec((1, D), lambda b, n, g: (0, 0)),
                  pl.BlockSpec((1, D), lambda b, n, g: (0, 0)),
                  pl.BlockSpec((gs, STATE2, CHUNK_W), lambda b, n, g: (g, 0, 0)),
                  pl.BlockSpec((gs, CHUNK_W, CHUNK_W + STATE2), lambda b, n, g: (g, 0, 0)),
                  pl.BlockSpec((gs, STATE2, ACOLS), lambda b, n, g: (g, 0, 0))],
        out_specs=[pl.BlockSpec((1, nk, CHUNK * D), lambda b, n, g: (b, n, 0)),
                   pl.BlockSpec((1, 1, gs, STATE2, 1), lambda b, n, g: (b, n, g, 0, 0))],
        out_shape=[jax.ShapeDtypeStruct((B, L // CHUNK, CHUNK * D), BF16),
                   jax.ShapeDtypeStruct((B, NB, G, STATE2, 1), F32)],
        scratch_shapes=[pltpu.VMEM((CHUNK, D, nk), BF16),
                        pltpu.VMEM((CHUNK, D, nk), BF16),
                        pltpu.VMEM((CHUNK, nk, 1), F32),
                        pltpu.VMEM((G, STATE2, 1), F32)],
        compiler_params=_cparams(("arbitrary", "arbitrary", "arbitrary")),
        name="s5_mix_prompt",
    )(*([x] * LANE_SLABS), pre_g.reshape(1, D), d_skip.reshape(1, D), wt, tm, acol)
    return z2, hout[:, NB - 1]


def _s5_mix_sample_kernel(x_ref, g_ref, d_ref, h0_ref, wt_ref, tm_ref, acol_ref, z_ref, hout_ref,
                          ut_ref, yt_ref, r_ref, *, nseq, ntok, gs):
    P, D = SSM_STATE, D_MODEL
    half = ntok * SSM_GROUP
    gg = pl.program_id(0)

    @pl.when(gg == 0)
    def _():
        for s in range(ntok):
            xs = x_ref[:, s * D:(s + 1) * D]
            r = _inv_rms(xs)
            r_ref[s] = r
            ut_ref[s] = (xs * r * g_ref[...]).T.astype(BF16)

    for gi in range(gs):
        g = gg * gs + gi
        row0 = pl.multiple_of(g * SSM_GROUP, SSM_GROUP)
        ug = ut_ref[:, pl.ds(row0, SSM_GROUP), :].reshape(half, nseq)
        h0 = h0_ref[gi]
        acol = acol_ref[gi]
        s_all = _dot(wt_ref[gi, :, half:], ug)
        er, ei = _cmul(acol[:P, ACOLS - 1:ACOLS], acol[P:, ACOLS - 1:ACOLS], h0[:P], h0[P:])
        hout_ref[gi] = jnp.concatenate([er + s_all[:P], ei + s_all[P:]], axis=0)
        y = _dot(tm_ref[gi, 0:half, 0:half], ug) + _dot(tm_ref[gi, 0:half, CHUNK_W:], h0.astype(BF16))
        yt_ref[:, pl.ds(row0, SSM_GROUP), :] = y.reshape(ntok, SSM_GROUP, nseq)

    @pl.when(gg == pl.num_programs(0) - 1)
    def _():
        for t in range(ntok):
            xs = x_ref[:, t * D:(t + 1) * D]
            u = xs * r_ref[t] * g_ref[...]
            v = yt_ref[t].T + d_ref[...] * u
            z_ref[:, t * D:(t + 1) * D] = _gelu(v).astype(BF16)


def _s5_mix_sample(x, h0, pre_g, d_skip, wt, tm, acol):
    S, T, D = x.shape
    assert T * 2 == CHUNK
    gs = 8
    G = N_GROUPS
    z2, hout = pl.pallas_call(
        functools.partial(_s5_mix_sample_kernel, nseq=S, ntok=T, gs=gs),
        grid=(G // gs,),
        in_specs=[pl.BlockSpec((S, T * D), lambda g: (0, 0)),
                  pl.BlockSpec((1, D), lambda g: (0, 0)),
                  pl.BlockSpec((1, D), lambda g: (0, 0)),
                  pl.BlockSpec((gs, STATE2, S), lambda g: (g, 0, 0)),
                  pl.BlockSpec((gs, STATE2, CHUNK_W), lambda g: (g, 0, 0)),
                  pl.BlockSpec((gs, CHUNK_W, CHUNK_W + STATE2), lambda g: (g, 0, 0)),
                  pl.BlockSpec((gs, STATE2, ACOLS), lambda g: (g, 0, 0))],
        out_specs=[pl.BlockSpec((S, T * D), lambda g: (0, 0)),
                   pl.BlockSpec((gs, STATE2, S), lambda g: (g, 0, 0))],
        out_shape=[jax.ShapeDtypeStruct((S, T * D), BF16),
                   jax.ShapeDtypeStruct((G, STATE2, S), F32)],
        scratch_shapes=[pltpu.VMEM((T, D, S), BF16),
                        pltpu.VMEM((T, D, S), F32),
                        pltpu.VMEM((T, S, 1), F32)],
        compiler_params=_cparams(("arbitrary",)),
        name="s5_mix_sample",
    )(x.reshape(S, T * D), pre_g.reshape(1, D), d_skip.reshape(1, D), h0, wt, tm, acol)
    return z2, hout


def _tail_kernel(a_ref, w_ref, g_ref, x_ref, o_ref, *, glu):
    y = _dot(a_ref[...], w_ref[...])
    if glu:
        n = y.shape[-1] // 2
        y = y[:, :n] * jax.nn.sigmoid(y[:, n:])
    o_ref[...] = x_ref[...] + y * _inv_rms(y) * g_ref[...]


def _tail(a2, w, g, x2, *, glu, tm):
    R = a2.shape[0]
    K = w.shape[0]
    D = D_MODEL
    n = a2.shape[1] // K
    tm = min(tm, R)
    assert R % tm == 0 and x2.shape == (R, n * D)
    return pl.pallas_call(
        functools.partial(_tail_kernel, glu=glu),
        grid=(R // tm, n),
        in_specs=[pl.BlockSpec((tm, K), lambda i, s: (i, s)),
                  _resident(w.shape),
                  pl.BlockSpec((1, D), lambda i, s: (0, 0)),
                  pl.BlockSpec((tm, D), lambda i, s: (i, s))],
        out_specs=pl.BlockSpec((tm, D), lambda i, s: (i, s)),
        out_shape=jax.ShapeDtypeStruct((R, n * D), F32),
        compiler_params=_cparams(("parallel", "parallel")),
        name="glu_tail" if glu else "oproj_tail",
    )(a2, w, g.reshape(1, D), x2)


def _glu_tail_prompt_kernel(z_ref, w_ref, g_ref, x_ref, o_ref, slab_ref, *, rk, ns):
    D = D_MODEL
    for s0 in range(0, CHUNK, ns):
        a = jnp.concatenate([z_ref[:, s * D:(s + 1) * D] for s in range(s0, s0 + ns)], axis=0)
        y = _dot(a, w_ref[...])
        y = y[:, :D] * jax.nn.sigmoid(y[:, D:])
        y = y * _inv_rms(y) * g_ref[...]
        for j in range(ns):
            for c in range(LANE_SLABS):
                slab_ref[c, pl.ds(s0 + j, rk, stride=CHUNK), :] = y[j * rk:(j + 1) * rk, c * LANES:(c + 1) * LANES]
    for c in range(LANE_SLABS):
        o_ref[:, c * LANES:(c + 1) * LANES] = x_ref[:, c * LANES:(c + 1) * LANES] + slab_ref[c]


def _glu_tail_prompt(z2, w, g, x2, *, rk, ns):
    R = z2.shape[0]
    D = D_MODEL
    assert R % rk == 0 and CHUNK % ns == 0 and x2.shape == (R * CHUNK, D)
    return pl.pallas_call(
        functools.partial(_glu_tail_prompt_kernel, rk=rk, ns=ns),
        grid=(R // rk,),
        in_specs=[pl.BlockSpec((rk, CHUNK * D), lambda i: (i, 0)),
                  _resident(w.shape),
                  pl.BlockSpec((1, D), lambda i: (0, 0)),
                  pl.BlockSpec((rk * CHUNK, D), lambda i: (i, 0))],
        out_specs=pl.BlockSpec((rk * CHUNK, D), lambda i: (i, 0)),
        out_shape=jax.ShapeDtypeStruct((R * CHUNK, D), F32),
        scratch_shapes=[pltpu.VMEM((LANE_SLABS, rk * CHUNK, LANES), F32)],
        compiler_params=_cparams(("parallel",)),
        name="glu_tail_prompt",
    )(z2, w, g.reshape(1, D), x2)


def _kvq_kernel(x_ref, gkv_ref, gq_ref, wk_ref, wv_ref, wq_ref, k_ref, v_ref, *q_refs, by_head, q_scale):
    x = x_ref[...]
    xr = x * _inv_rms(x)
    kv_in = (xr * gkv_ref[...]).astype(BF16)
    xn = (xr * gq_ref[...]).astype(BF16)
    k = _dot(kv_in, wk_ref[...])
    v = _dot(kv_in, wv_ref[...])
    q = _dot(xn, wq_ref[...]) * q_scale
    if by_head:
        for ref, val in ((k_ref, k), (v_ref, v), (q_refs[0], q)):
            for h in range(N_HEADS):
                ref[:, h, :] = val[:, h * V_DIM:(h + 1) * V_DIM]
    else:
        kb_ref, vb_ref, qb_ref = q_refs
        k_ref[...] = k
        v_ref[...] = v
        kb_ref[...] = k.astype(BF16)
        vb_ref[...] = v.astype(BF16)
        qb_ref[...] = q.astype(BF16)


def _kvq(x2, g_kv, g_q, wk, wv, wq, *, tm, by_head, q_scale):
    R, D = x2.shape
    tm = min(tm, R)
    assert R % tm == 0
    tile = pl.BlockSpec((tm, D), lambda i: (i, 0))
    heads = pl.BlockSpec((tm, N_HEADS, V_DIM), lambda i: (i, 0, 0))
    vec = pl.BlockSpec((1, D), lambda i: (0, 0))
    head32 = jax.ShapeDtypeStruct((R, N_HEADS, V_DIM), F32)
    flat32 = jax.ShapeDtypeStruct((R, D), F32)
    flat16 = jax.ShapeDtypeStruct((R, D), BF16)
    return pl.pallas_call(
        functools.partial(_kvq_kernel, by_head=by_head, q_scale=q_scale),
        grid=(R // tm,),
        in_specs=[tile, vec, vec, _resident(wk.shape), _resident(wv.shape), _resident(wq.shape)],
        out_specs=[heads] * 3 if by_head else [tile] * 5,
        out_shape=[head32] * 3 if by_head else [flat32, flat32, flat16, flat16, flat16],
        compiler_params=_cparams(("parallel",)),
        name="kvq_proj",
    )(x2, g_kv.reshape(1, D), g_q.reshape(1, D), wk, wv, wq)


FF_CHUNK = D_FF // 2


def _ffn_cols(c):
    return (slice(c * FF_CHUNK, (c + 1) * FF_CHUNK),
            slice(D_FF + c * FF_CHUNK, D_FF + (c + 1) * FF_CHUNK))


def _ffn_prompt_kernel(x_ref, gpre_ref, gpost_ref, wup_ref, cw_ref, cb_ref, wdn_ref,
                       o_ref, cs_ref, carry_ref, *, tm):
    @pl.when(pl.program_id(1) == 0)
    def _():
        carry_ref[...] = jnp.zeros_like(carry_ref)

    x = x_ref[0]
    xn = (x * _inv_rms(x) * gpre_ref[...]).astype(BF16)
    rows = lax.broadcasted_iota(jnp.int32, (tm, 1), 0)
    f = jnp.zeros((tm, D_MODEL), F32)
    for c in range(D_FF // FF_CHUNK):
        halves = []
        for cols in _ffn_cols(c):
            h = _dot(xn, wup_ref[:, cols])
            c0, c1 = carry_ref[6:7, cols], carry_ref[7:8, cols]
            h1 = jnp.where(rows == 0, c1, pltpu.roll(h, 1, axis=0))
            h2 = jnp.where(rows == 0, c0, jnp.where(rows == 1, c1, pltpu.roll(h, 2, axis=0)))
            halves.append(cb_ref[:, cols] + cw_ref[2:3, cols] * h
                          + cw_ref[1:2, cols] * h1 + cw_ref[0:1, cols] * h2)
            carry_ref[:, cols] = h[tm - 8:tm]
            cs_ref[0, :, cols] = h[tm - 2:tm]
        a = (_gelu(halves[0]) * halves[1]).astype(BF16)
        f = f + _dot(a, wdn_ref[c * FF_CHUNK:(c + 1) * FF_CHUNK, :])
    o_ref[0] = x + f * _inv_rms(f) * gpost_ref[...]


def _ffn_prompt(x, gpre, gpost, wup, cw, cb, wdn, *, tm):
    B, L, D = x.shape
    tm = min(tm, L)
    assert L % tm == 0
    F2 = 2 * D_FF
    vec = pl.BlockSpec((1, D), lambda b, t: (0, 0))
    return pl.pallas_call(
        functools.partial(_ffn_prompt_kernel, tm=tm),
        grid=(B, L // tm),
        in_specs=[pl.BlockSpec((1, tm, D), lambda b, t: (b, t, 0)), vec, vec,
                  _resident(wup.shape), _resident((CONV_W, F2)), _resident((1, F2)),
                  _resident(wdn.shape)],
        out_specs=[pl.BlockSpec((1, tm, D), lambda b, t: (b, t, 0)),
                   pl.BlockSpec((1, CONV_W - 1, F2), lambda b, t: (b, 0, 0))],
        out_shape=[jax.ShapeDtypeStruct((B, L, D), F32),
                   jax.ShapeDtypeStruct((B, CONV_W - 1, F2), F32)],
        scratch_shapes=[pltpu.VMEM((8, F2), F32)],
        compiler_params=_cparams(("arbitrary", "arbitrary")),
        name="ffn_prompt",
    )(x, gpre.reshape(1, D), gpost.reshape(1, D), wup, cw, cb.reshape(1, F2), wdn)


def _ffn_sample_kernel(x_ref, buf_ref, gpre_ref, gpost_ref, wup_ref, cw_ref, cb_ref, wdn_ref,
                       o_ref, cs_ref, *, tm, ntok):
    nsq = tm // ntok
    x = x_ref[...]
    xn = (x * _inv_rms(x) * gpre_ref[...]).astype(BF16)
    sub = lax.broadcasted_iota(jnp.int32, (tm, 1), 0) % ntok
    f = jnp.zeros((tm, D_MODEL), F32)
    for c in range(D_FF // FF_CHUNK):
        halves = []
        for cols in _ffn_cols(c):
            h = _dot(xn, wup_ref[:, cols])
            spread = lambda b: jnp.broadcast_to(b, (nsq, ntok, FF_CHUNK)).reshape(tm, FF_CHUNK)
            b0, b1 = spread(buf_ref[:, 0:1, cols]), spread(buf_ref[:, 1:2, cols])
            h1 = jnp.where(sub == 0, b1, pltpu.roll(h, 1, axis=0))
            h2 = jnp.where(sub == 0, b0, jnp.where(sub == 1, b1, pltpu.roll(h, 2, axis=0)))
            halves.append(cb_ref[:, cols] + cw_ref[2:3, cols] * h
                          + cw_ref[1:2, cols] * h1 + cw_ref[0:1, cols] * h2)
            cs_ref[:, :, cols] = h.reshape(nsq, ntok, FF_CHUNK)[:, ntok - 2:ntok, :]
        a = (_gelu(halves[0]) * halves[1]).astype(BF16)
        f = f + _dot(a, wdn_ref[c * FF_CHUNK:(c + 1) * FF_CHUNK, :])
    o_ref[...] = x + f * _inv_rms(f) * gpost_ref[...]


def _ffn_sample(x, buf, gpre, gpost, wup, cw, cb, wdn, *, tm):
    S, T, D = x.shape
    assert T == 8 and tm % T == 0 and (S * T) % tm == 0
    F2 = 2 * D_FF
    nsq = tm // T
    vec = pl.BlockSpec((1, D), lambda i: (0, 0))
    out, cs = pl.pallas_call(
        functools.partial(_ffn_sample_kernel, tm=tm, ntok=T),
        grid=(S * T // tm,),
        in_specs=[pl.BlockSpec((tm, D), lambda i: (i, 0)),
                  pl.BlockSpec((nsq, CONV_W - 1, F2), lambda i: (i, 0, 0)), vec, vec,
                  _resident(wup.shape), _resident((CONV_W, F2)), _resident((1, F2)),
                  _resident(wdn.shape)],
        out_specs=[pl.BlockSpec((tm, D), lambda i: (i, 0)),
                   pl.BlockSpec((nsq, CONV_W - 1, F2), lambda i: (i, 0, 0))],
        out_shape=[jax.ShapeDtypeStruct((S * T, D), F32),
                   jax.ShapeDtypeStruct((S, CONV_W - 1, F2), F32)],
        compiler_params=_cparams(("parallel",)),
        name="ffn_sample",
    )(x.reshape(S * T, D), buf, gpre.reshape(1, D), gpost.reshape(1, D), wup, cw,
      cb.reshape(1, F2), wdn)
    return out.reshape(S, T, D), cs


def _lam_value(lq1, lk1, lq2, lk2, lam_init):
    return (jnp.exp(jnp.sum(lq1[...] * lk1[...], axis=-1, keepdims=True))
            - jnp.exp(jnp.sum(lq2[...] * lk2[...], axis=-1, keepdims=True)) + lam_init)


def _sub_norm(o, subg, lam_init):
    return o * _inv_rms(o) * subg * (1.0 - lam_init)


LOG2E = math.log2(math.e)
FLASH_ROWS = 256
POS_SPLIT = 64


def _flash_kernel(qi_ref, ki_ref, q_ref, k_ref, v_ref, lq1, lk1, lq2, lk2, subg_ref, o_ref,
                  q2_ref, kf_ref, m_ref, acc_ref, *, tq, lam_init):
    h, step = pl.program_id(1), pl.program_id(2)
    qi, ki = qi_ref[step], ki_ref[step]
    RQ = FLASH_ROWS
    n_chunks = tq // RQ
    lane = lax.broadcasted_iota(jnp.int32, (1, V_DIM), 1)
    slope2 = jnp.exp2(-jnp.full((1, V_DIM), h + 1, jnp.int32).astype(F32)) * LOG2E

    @pl.when(ki == 0)
    def _():
        m_ref[...] = jnp.full_like(m_ref, -jnp.inf)
        acc_ref[...] = jnp.zeros_like(acc_ref)
        kidx = lax.broadcasted_iota(jnp.int32, (tq, 1), 0)
        a, b = (kidx // POS_SPLIT).astype(F32), (kidx % POS_SPLIT).astype(F32)
        kf_ref[...] = jnp.where(lane < 3, a, jnp.where(lane < 6, b, 0.0)).astype(BF16)
        c0 = slope2.astype(BF16).astype(F32)
        c1 = (slope2 - c0).astype(BF16).astype(F32)
        c2 = (slope2 - c0 - c1).astype(BF16).astype(F32)
        part = jnp.where(lane % 3 == 0, c0, jnp.where(lane % 3 == 1, c1, c2))
        qf = jnp.where(lane < 3, POS_SPLIT * part, jnp.where(lane < 6, part, 0.0)).astype(BF16)
        qf = jnp.broadcast_to(qf, (RQ, V_DIM))
        for c_i in range(n_chunks):
            q = q_ref[c_i * RQ:(c_i + 1) * RQ, :]
            zero = jnp.zeros_like(q)
            q2_ref[c_i, 0:RQ, 0:V_DIM] = jnp.where(lane < HEAD_DIM, q, zero)
            q2_ref[c_i, RQ:2 * RQ, 0:V_DIM] = jnp.where(lane >= HEAD_DIM, q, zero)
            q2_ref[c_i, 0:RQ, V_DIM:] = qf
            q2_ref[c_i, RQ:2 * RQ, V_DIM:] = qf

    def absorb(diagonal):
        k_aug = jnp.concatenate([k_ref[...], kf_ref[...]], axis=1)
        v_aug = jnp.concatenate([v_ref[...], jnp.ones((tq, V_DIM), BF16)], axis=1)
        shift = slope2 * ((ki - qi) * tq).astype(F32)
        for c_i in range(n_chunks):
            nkeys = (c_i + 1) * RQ if diagonal else tq
            s = lax.dot_general(q2_ref[c_i], k_aug[0:nkeys], (((1,), (1,)), ((), ())),
                                preferred_element_type=F32)
            if diagonal:
                kcol = lax.broadcasted_iota(jnp.int32, (1, RQ), 1)
                qrow = lax.broadcasted_iota(jnp.int32, (2 * RQ, 1), 0) % RQ
                tail = jnp.where(kcol <= qrow, s[:, nkeys - RQ:], -jnp.inf)
                s = tail if nkeys == RQ else jnp.concatenate([s[:, 0:nkeys - RQ], tail], axis=1)
            m_old = m_ref[c_i]
            m_new = jnp.maximum(m_old, jnp.max(s, axis=-1, keepdims=True) + shift)
            alpha = jnp.exp2(m_old - m_new)
            p = jnp.exp2(s - jnp.concatenate([m_new - shift] * (nkeys // V_DIM), axis=1))
            pv = _dot(p.astype(BF16), v_aug[0:nkeys])
            acc_ref[c_i] = jnp.concatenate([alpha, alpha], axis=1) * acc_ref[c_i] + pv
            m_ref[c_i] = m_new

    @pl.when(ki < qi)
    def _():
        absorb(False)

    @pl.when(ki == qi)
    def _():
        absorb(True)
        lam = _lam_value(lq1, lk1, lq2, lk2, lam_init)
        for c_i in range(n_chunks):
            acc = acc_ref[c_i]
            o = (acc[0:RQ, 0:V_DIM] / acc[0:RQ, V_DIM:]
                 - lam * (acc[RQ:, 0:V_DIM] / acc[RQ:, V_DIM:]))
            o_ref[c_i * RQ:(c_i + 1) * RQ, :] = _sub_norm(o, subg_ref[...], lam_init).astype(o_ref.dtype)


def _flash_prompt(qb, kb, vb, lq1, lk1, lq2, lk2, subg, *, B, L, tq, lam_init):
    tq = min(tq, L)
    assert L % tq == 0 and tq % FLASH_ROWS == 0 and tq <= POS_SPLIT * POS_SPLIT
    nq = L // tq
    n_chunks = tq // FLASH_ROWS
    pairs = [(i, j) for i in range(nq) for j in range(i + 1)]
    qi_tab = jnp.asarray([p[0] for p in pairs], jnp.int32)
    ki_tab = jnp.asarray([p[1] for p in pairs], jnp.int32)
    lvec = pl.BlockSpec((1, HEAD_DIM), lambda b, h, s, qt, kt: (0, 0))
    grid_spec = pltpu.PrefetchScalarGridSpec(
        num_scalar_prefetch=2,
        grid=(B, N_HEADS, len(pairs)),
        in_specs=[pl.BlockSpec((tq, V_DIM), lambda b, h, s, qt, kt: (b * nq + qt[s], h)),
                  pl.BlockSpec((tq, V_DIM), lambda b, h, s, qt, kt: (b * nq + kt[s], h)),
                  pl.BlockSpec((tq, V_DIM), lambda b, h, s, qt, kt: (b * nq + kt[s], h)),
                  lvec, lvec, lvec, lvec,
                  pl.BlockSpec((1, V_DIM), lambda b, h, s, qt, kt: (0, 0))],
        out_specs=pl.BlockSpec((tq, V_DIM), lambda b, h, s, qt, kt: (b * nq + qt[s], h)),
        scratch_shapes=[pltpu.VMEM((n_chunks, 2 * FLASH_ROWS, 2 * V_DIM), BF16),
                        pltpu.VMEM((tq, V_DIM), BF16),
                        pltpu.VMEM((n_chunks, 2 * FLASH_ROWS, V_DIM), F32),
                        pltpu.VMEM((n_chunks, 2 * FLASH_ROWS, 2 * V_DIM), F32)])
    return pl.pallas_call(
        functools.partial(_flash_kernel, tq=tq, lam_init=lam_init),
        grid_spec=grid_spec,
        out_shape=jax.ShapeDtypeStruct((B * L, N_HEADS * V_DIM), BF16),
        compiler_params=_cparams(("parallel", "parallel", "arbitrary")),
        name="flash_prompt",
    )(qi_tab, ki_tab, qb, kb, vb, lq1, lk1, lq2, lk2, subg)


def _paged_kernel(pt_ref, q_ref, *refs, ntok, page, npp, past_len, lam_init):
    kc = refs[:npp]
    vc = refs[npp:2 * npp]
    (kn_ref, vn_ref, lq1, lk1, lq2, lk2, subg_ref, o_ref,
     qall_ref, bias_ref, m_ref, l_ref, acc_ref) = refs[2 * npp:]
    H = N_HEADS
    R = 2 * ntok * H
    PW = page * H
    pg = pl.program_id(1)

    row = lax.broadcasted_iota(jnp.int32, (R, 1), 0)
    slope_r = jnp.exp2(-(row % H + 1).astype(F32))
    qi_r = (row // H) % ntok

    def head_bias(n_lanes, key_limit):
        lane = lax.broadcasted_iota(jnp.int32, (1, n_lanes), 1)
        key = lane // H
        ok = (lane % H == row % H) & (key < key_limit)
        return jnp.where(ok, slope_r * (key - qi_r).astype(F32), -jnp.inf)

    @pl.when(pg == 0)
    def _():
        m_ref[...] = jnp.full_like(m_ref, -jnp.inf)
        l_ref[...] = jnp.zeros_like(l_ref)
        acc_ref[...] = jnp.zeros_like(acc_ref)
        q2 = q_ref[...].reshape(ntok * H, V_DIM)
        col = lax.broadcasted_iota(jnp.int32, (1, V_DIM), 1)
        qall_ref[...] = jnp.concatenate([jnp.where(col < HEAD_DIM, q2, 0.0),
                                         jnp.where(col >= HEAD_DIM, q2, 0.0)], axis=0).astype(BF16)
        bias_ref[...] = head_bias(PW, page)

    def absorb(ks, vs, biases):
        qall = qall_ref[...]
        s = [lax.dot_general(qall, k, (((1,), (1,)), ((), ())), preferred_element_type=F32) + b
             for k, b in zip(ks, biases)]
        m_old = m_ref[...]
        m_new = m_old
        for si in s:
            m_new = jnp.maximum(m_new, jnp.max(si, axis=-1, keepdims=True))
        alpha = jnp.exp(m_old - m_new)
        l_new = alpha * l_ref[...]
        acc = alpha * acc_ref[...]
        for si, v in zip(s, vs):
            p = jnp.exp(si - m_new)
            l_new = l_new + jnp.sum(p, axis=-1, keepdims=True)
            acc = acc + _dot(p.astype(BF16), v)
        m_ref[...] = m_new
        l_ref[...] = l_new
        acc_ref[...] = acc

    flat = lambda ref: ref[0].reshape(PW, V_DIM).astype(BF16)
    biases = [bias_ref[...] + slope_r * ((pg * npp + i) * page - past_len).astype(F32) for i in range(npp)]
    absorb([flat(r) for r in kc], [flat(r) for r in vc], biases)

    @pl.when(pg == pl.num_programs(1) - 1)
    def _():
        n_new = ntok * H
        pad = jnp.zeros((R - n_new, V_DIM), F32)
        kn = jnp.concatenate([kn_ref[...].reshape(n_new, V_DIM), pad], axis=0).astype(BF16)
        vn = jnp.concatenate([vn_ref[...].reshape(n_new, V_DIM), pad], axis=0).astype(BF16)
        absorb([kn], [vn], [head_bias(R, jnp.minimum(qi_r + 1, ntok))])
        lam = _lam_value(lq1, lk1, lq2, lk2, lam_init)
        h = R // 2
        o = acc_ref[0:h] / l_ref[0:h] - lam * (acc_ref[h:R] / l_ref[h:R])
        o_ref[...] = _sub_norm(o, subg_ref[...], lam_init).reshape(ntok, H, V_DIM)


def _paged_sample(page_table, q, cache_k, cache_v, k_new, v_new, lq1, lk1, lq2, lk2, subg,
                  *, ntok, npp, lam_init):
    S, n_pages = page_table.shape
    page = cache_k.shape[1]
    H = N_HEADS
    R = 2 * ntok * H
    assert n_pages % npp == 0
    tok = pl.BlockSpec((ntok, H, V_DIM), lambda s, p, pt: (s, 0, 0))
    pgs = [pl.BlockSpec((1, page, H, V_DIM), functools.partial(
        lambda s, p, pt, i: (pt[s, p * npp + i], 0, 0, 0), i=i)) for i in range(npp)]
    lvec = pl.BlockSpec((1, HEAD_DIM), lambda s, p, pt: (0, 0))
    grid_spec = pltpu.PrefetchScalarGridSpec(
        num_scalar_prefetch=1,
        grid=(S, n_pages // npp),
        in_specs=[tok] + pgs + pgs + [tok, tok, lvec, lvec, lvec, lvec,
                                      pl.BlockSpec((1, V_DIM), lambda s, p, pt: (0, 0))],
        out_specs=tok,
        scratch_shapes=[pltpu.VMEM((R, V_DIM), BF16), pltpu.VMEM((R, page * H), F32),
                        pltpu.VMEM((R, 1), F32), pltpu.VMEM((R, 1), F32), pltpu.VMEM((R, V_DIM), F32)])
    return pl.pallas_call(
        functools.partial(_paged_kernel, ntok=ntok, page=page, npp=npp, past_len=n_pages * page,
                          lam_init=lam_init),
        grid_spec=grid_spec,
        out_shape=jax.ShapeDtypeStruct((S * ntok, H, V_DIM), F32),
        compiler_params=_cparams(("arbitrary", "arbitrary")),
        name="paged_sample",
    )(page_table, q, *([cache_k] * npp), *([cache_v] * npp), k_new, v_new, lq1, lk1, lq2, lk2, subg)


def kernel(x_prompt, x_sample, state_ssm_re, state_ssm_im, state_conv, cache_k, cache_v, page_table,
           a_pre_g, a_post_g, ssm_lam_re, ssm_lam_im, ssm_log_dt, ssm_b_re, ssm_b_im, ssm_c_re,
           ssm_c_im, ssm_d, glu_w, kv_norm_g, w_k, w_v, b_pre_g, b_post_g, w_q, lam_q1, lam_k1,
           lam_q2, lam_k2, sub_g, w_o, f_pre_g, f_post_g, w_up, conv_w, conv_b, w_down):
    B, L, D = x_prompt.shape
    S, T, _ = x_sample.shape
    P, G = SSM_STATE, N_GROUPS
    HW = N_HEADS * V_DIM
    st_dtype = state_ssm_re.dtype

    wt, tm_op, acol = _s5_prepare(ssm_lam_re[0], ssm_lam_im[0], ssm_log_dt[0], ssm_b_re[0],
                                  ssm_b_im[0], ssm_c_re[0], ssm_c_im[0])
    glu_b = glu_w[0].astype(BF16)
    zp, hp = _s5_mix_prompt(x_prompt, a_pre_g[0], ssm_d[0], wt, tm_op, acol)
    xp = _glu_tail_prompt(zp.reshape(B * L // CHUNK, CHUNK * D), glu_b, a_post_g[0],
                          x_prompt.reshape(B * L, D), rk=64, ns=4).reshape(B, L, D)
    h0 = jnp.concatenate([state_ssm_re[0].astype(F32), state_ssm_im[0].astype(F32)], axis=-1)
    zs, hs = _s5_mix_sample(x_sample, h0.transpose(1, 2, 0), a_pre_g[0], ssm_d[0], wt, tm_op, acol)
    xs = _tail(zs, glu_b, a_post_g[0], x_sample.reshape(S, T * D), glu=True, tm=512).reshape(S, T, D)

    hp = hp.reshape(B, G, 2 * P)
    hs = hs.transpose(2, 0, 1)
    ssm_re_p, ssm_im_p = hp[None, ..., :P].astype(st_dtype), hp[None, ..., P:].astype(st_dtype)
    ssm_re_s, ssm_im_s = hs[None, ..., :P].astype(st_dtype), hs[None, ..., P:].astype(st_dtype)

    wup_b, wdn_b = w_up.astype(BF16), w_down.astype(BF16)
    xp, conv_p0 = _ffn_prompt(xp, f_pre_g[0], f_post_g[0], wup_b[0], conv_w[0], conv_b[0], wdn_b[0], tm=256)
    xs, conv_s0 = _ffn_sample(xs, state_conv[0], f_pre_g[0], f_post_g[0], wup_b[0], conv_w[0],
                              conv_b[0], wdn_b[0], tm=256)

    lam_init = 0.8 - 0.6 * math.exp(-0.3 * N_A_LAYERS)
    wk_b, wv_b, wq_b, wo_b = w_k.astype(BF16), w_v.astype(BF16), w_q[0].astype(BF16), w_o[0].astype(BF16)
    lvec = lambda a: a[0].reshape(1, HEAD_DIM).astype(F32)
    lams = (lvec(lam_q1), lvec(lam_k1), lvec(lam_q2), lvec(lam_k2))
    subg = sub_g[0].reshape(1, V_DIM)

    xp2 = xp.reshape(B * L, D)
    kp, vp, kpb, vpb, qpb = _kvq(xp2, kv_norm_g, b_pre_g[0], wk_b, wv_b, wq_b, tm=512, by_head=False,
                                 q_scale=LOG2E * HEAD_DIM ** -0.5)
    op = _flash_prompt(qpb, kpb, vpb, *lams, subg, B=B, L=L, tq=2048, lam_init=lam_init)
    xp = _tail(op, wo_b, b_post_g[0], xp2, glu=False, tm=512).reshape(B, L, D)

    xs2 = xs.reshape(S * T, D)
    ks, vs, qs = _kvq(xs2, kv_norm_g, b_pre_g[0], wk_b, wv_b, wq_b, tm=512, by_head=True,
                      q_scale=HEAD_DIM ** -0.5)
    os_ = _paged_sample(page_table, qs, cache_k, cache_v, ks, vs, *lams, subg, ntok=T, npp=8,
                        lam_init=lam_init)
    xs = _tail(os_.reshape(S * T, HW).astype(BF16), wo_b, b_post_g[0], xs2, glu=False, tm=512).reshape(S, T, D)

    xp, conv_p1 = _ffn_prompt(xp, f_pre_g[1], f_post_g[1], wup_b[1], conv_w[1], conv_b[1], wdn_b[1], tm=256)
    xs, conv_s1 = _ffn_sample(xs, state_conv[1], f_pre_g[1], f_post_g[1], wup_b[1], conv_w[1],
                              conv_b[1], wdn_b[1], tm=256)

    return (xp, xs, ssm_re_p, ssm_im_p, jnp.stack([conv_p0, conv_p1]),
            kp.reshape(B, L, N_HEADS, 2 * HEAD_DIM), vp.reshape(B, L, N_HEADS, V_DIM),
            ssm_re_s, ssm_im_s, jnp.stack([conv_s0, conv_s1]),
            ks.reshape(S, T, N_HEADS, 2 * HEAD_DIM), vs.reshape(S, T, N_HEADS, V_DIM))
```

```python
import functools
import math

import jax
import jax.numpy as jnp
from jax import lax
from jax.experimental import pallas as pl
from jax.experimental.pallas import tpu as pltpu

D_MODEL = 1024
SSM_GROUP = 16
N_GROUPS = D_MODEL // SSM_GROUP
SSM_STATE = 64
N_HEADS = 8
HEAD_DIM = 64
V_DIM = 2 * HEAD_DIM
D_FF = 2816
CONV_W = 3
NORM_EPS = 1e-6
N_A_LAYERS = 1

LANES = 128
LANE_SLABS = D_MODEL // LANES
CHUNK = 16
STATE2 = 2 * SSM_STATE
CHUNK_W = CHUNK * SSM_GROUP
SCAN_STEPS = 7
ACOLS = 8
HIGHEST = lax.Precision.HIGHEST
BF16 = jnp.bfloat16
F32 = jnp.float32
VMEM_LIMIT = 56 * 1024 * 1024


def _cparams(sem, vmem=VMEM_LIMIT):
    return pltpu.CompilerParams(dimension_semantics=sem, vmem_limit_bytes=vmem)


def _resident(shape):
    zeros = (0,) * len(shape)
    return pl.BlockSpec(shape, lambda *_: zeros, pipeline_mode=pl.Buffered(1))


def _inv_rms(x):
    return lax.rsqrt(jnp.mean(x * x, axis=-1, keepdims=True) + NORM_EPS)


def _gelu(x):
    c = math.sqrt(2.0 / math.pi)
    return x * (0.5 * (1.0 + jnp.tanh(c * (x + 0.044715 * (x * x * x)))))


def _dot(a, b):
    return jnp.dot(a, b, preferred_element_type=F32)


def _s5_prep_kernel(lre, lim, ldt, bre, bim, cre, cim, w_ref, z_ref, kmat_ref, acol_ref):
    P = SSM_STATE
    dt = jnp.exp(ldt[...])
    lr, li = lre[...], lim[...]
    ar, ai = lr * dt, li * dt
    first = lax.broadcasted_iota(jnp.int32, (1, 2 * P), 1) < P
    sgn = jnp.where(first, -1.0, 1.0)

    def cpow(n):
        m = jnp.exp(n * ar)
        pr, pi = m * jnp.cos(n * ai), m * jnp.sin(n * ai)
        return jnp.where(first, pr, pi), jnp.where(first, pi, pr)

    def cmul(x1, x2, yr, yi):
        return x1[:, None, :] * yr + (sgn * x2)[:, None, :] * yi

    lbr, lbi = jnp.exp(ar) * jnp.cos(ai), jnp.exp(ar) * jnp.sin(ai)
    den = lr * lr + li * li
    nr, ni = lbr - 1.0, lbi
    cr = (nr * lr + ni * li) / den
    ci = (ni * lr - nr * li) / den
    bbr = cr[:, None, :] * bre[...] - ci[:, None, :] * bim[...]
    bbi = cr[:, None, :] * bim[...] + ci[:, None, :] * bre[...]
    for s in range(CHUNK):
        w_ref[s] = cmul(*cpow(float(CHUNK - 1 - s)), bbr, bbi)
    for i in range(ACOLS):
        n = float(CHUNK * 2 ** i) if i < SCAN_STEPS else float(CHUNK // 2)
        acol_ref[i] = cpow(n)[0]
    for m in range(CHUNK + 1):
        z_ref[m] = -sgn * cmul(*cpow(float(m)), cre[...], cim[...])
    z_all = jnp.concatenate([z_ref[m] for m in range(CHUNK)], axis=1)
    bb = jnp.where(first, bbr, bbi)
    kmat_ref[...] = lax.dot_general(z_all, bb, (((2,), (2,)), ((0,), (0,))), precision=HIGHEST,
                                    preferred_element_type=F32)


def _s5_prepare(lam_re, lam_im, log_dt, b_re, b_im, c_re, c_im):
    G, P, C = N_GROUPS, SSM_STATE, SSM_GROUP
    whole = lambda *shape: pl.BlockSpec(shape, lambda i: (0,) * len(shape))
    dup = lambda a: jnp.concatenate([a, a], axis=-1)
    g2, gc2 = (G, 2 * P), (G, C, 2 * P)
    outs = [(CHUNK,) + gc2, (CHUNK + 1,) + gc2, (G, CHUNK_W, C), (ACOLS,) + g2]
    w, z, kmat, acol = pl.pallas_call(
        _s5_prep_kernel,
        grid=(1,),
        in_specs=[whole(*g2), whole(*g2), whole(G, 1), whole(*gc2), whole(*gc2), whole(*gc2), whole(*gc2)],
        out_specs=[whole(*s) for s in outs],
        out_shape=[jax.ShapeDtypeStruct(s, F32) for s in outs],
        compiler_params=_cparams(("arbitrary",)),
        name="s5_prep",
    )(dup(lam_re), dup(lam_im), log_dt.reshape(G, 1), dup(b_re.transpose(0, 2, 1)),
      dup(b_im.transpose(0, 2, 1)), dup(c_re), dup(c_im))
    wt = w.transpose(1, 3, 0, 2).reshape(G, STATE2, CHUNK_W)
    k4 = kmat.reshape(G, CHUNK, C, C)
    t_idx = jnp.arange(CHUNK)[:, None] - jnp.arange(CHUNK)[None, :]
    toe = jnp.where((t_idx >= 0)[None, :, :, None, None],
                    k4[:, jnp.clip(t_idx, 0, CHUNK - 1)], 0.0)
    toe = toe.transpose(0, 1, 3, 2, 4).reshape(G, CHUNK_W, CHUNK_W)
    m_op = z[1:].transpose(1, 0, 2, 3).reshape(G, CHUNK_W, STATE2)
    tm = jnp.concatenate([toe, m_op], axis=-1)
    return wt.astype(BF16), tm.astype(BF16), acol.transpose(1, 2, 0)


def _cmul(ar, ai, br, bi):
    return ar * br - ai * bi, ar * bi + ai * br


def _s5_mix_prompt_kernel(*refs, nk, gs):
    x_refs = refs[:LANE_SLABS]
    (g_ref, d_ref, wt_ref, tm_ref, acol_ref, z_ref, hout_ref,
     ut_ref, yt_ref, r_ref, carry_ref) = refs[LANE_SLABS:]
    P, D = SSM_STATE, D_MODEL
    nb, gg = pl.program_id(1), pl.program_id(2)

    def slot(s):
        return jnp.concatenate([x[0, pl.ds(s, nk, stride=CHUNK), :] for x in x_refs], axis=1)

    @pl.when(gg == 0)
    def _():
        @pl.when(nb == 0)
        def _():
            carry_ref[...] = jnp.zeros_like(carry_ref)
        for s in range(CHUNK):
            xs = slot(s)
            r = _inv_rms(xs)
            r_ref[s] = r
            ut_ref[s] = (xs * r * g_ref[...]).T.astype(BF16)

    lane = lax.broadcasted_iota(jnp.int32, (gs * P, nk), 1)
    row0 = pl.multiple_of(gg * (gs * SSM_GROUP), gs * SSM_GROUP)
    u_all = ut_ref[:, pl.ds(row0, gs * SSM_GROUP), :]
    ugs = [u_all[:, gi * SSM_GROUP:(gi + 1) * SSM_GROUP, :].reshape(CHUNK_W, nk) for gi in range(gs)]
    s_loc = [_dot(wt_ref[gi], ugs[gi]) for gi in range(gs)]
    stack = lambda parts: jnp.concatenate(parts, axis=0)
    hin = carry_ref[pl.ds(gg * gs, gs)]
    hin_r, hin_i = hin[:, :P].reshape(gs * P, 1), hin[:, P:].reshape(gs * P, 1)
    acol = acol_ref[...]
    a_r, a_i = acol[:, :P].reshape(gs * P, ACOLS), acol[:, P:].reshape(gs * P, ACOLS)
    cr, ci = _cmul(a_r[:, 0:1], a_i[:, 0:1], hin_r, hin_i)
    sr = stack([s[:P] for s in s_loc]) + jnp.where(lane == 0, cr, 0.0)
    si = stack([s[P:] for s in s_loc]) + jnp.where(lane == 0, ci, 0.0)
    for i in range(SCAN_STEPS):
        sh = 1 << i
        pr = jnp.where(lane >= sh, pltpu.roll(sr, sh, axis=1), 0.0)
        pi = jnp.where(lane >= sh, pltpu.roll(si, sh, axis=1), 0.0)
        qr, qi = _cmul(a_r[:, i:i + 1], a_i[:, i:i + 1], pr, pi)
        sr, si = sr + qr, si + qi
    hpr = jnp.where(lane >= 1, pltpu.roll(sr, 1, axis=1), hin_r)
    hpi = jnp.where(lane >= 1, pltpu.roll(si, 1, axis=1), hin_i)
    end_r, end_i = sr[:, nk - 1:nk].reshape(gs, P, 1), si[:, nk - 1:nk].reshape(gs, P, 1)
    hend = jnp.concatenate([end_r, end_i], axis=1)
    carry_ref[pl.ds(gg * gs, gs)] = hend
    hout_ref[0, 0] = hend
    ys = []
    for gi in range(gs):
        rows = slice(gi * P, (gi + 1) * P)
        hprev = stack([hpr[rows], hpi[rows]]).astype(BF16)
        y = _dot(tm_ref[gi, :, 0:CHUNK_W], ugs[gi]) + _dot(tm_ref[gi, :, CHUNK_W:], hprev)
        ys.append(y.reshape(CHUNK, SSM_GROUP, nk).astype(BF16))
    yt_ref[:, pl.ds(row0, gs * SSM_GROUP), :] = jnp.concatenate(ys, axis=1)

    @pl.when(gg == pl.num_programs(2) - 1)
    def _():
        for t in range(CHUNK):
            u = slot(t) * r_ref[t] * g_ref[...]
            v = yt_ref[t].astype(F32).T + d_ref[...] * u
            z_ref[0, :, t * D:(t + 1) * D] = _gelu(v).astype(BF16)


def _s5_mix_prompt(x, pre_g, d_skip, wt, tm, acol):
    B, L, D = x.shape
    nk = 1 << SCAN_STEPS
    blk = nk * CHUNK
    assert L % blk == 0
    NB = L // blk
    gs = 8
    G = N_GROUPS
    slab = lambda c: pl.BlockSpec((1, blk, LANES), lambda b, n, g: (b, n, c))
    z2, hout = pl.pallas_call(
        functools.partial(_s5_mix_prompt_kernel, nk=nk, gs=gs),
        grid=(B, NB, G // gs),
        in_specs=[slab(c) for c in range(LANE_SLABS)] + [
                  pl.BlockSpec((1, D), lambda b, n, g: (0, 0)),
                  pl.BlockSpec((1, D), lambda b, n, g: (0, 0)),
                  pl.BlockSpec((gs, STATE2, CHUNK_W), lambda b, n, g: (g, 0, 0)),
                  pl.BlockSpec((gs, CHUNK_W, CHUNK_W + STATE2), lambda b, n, g: (g, 0, 0)),
                  pl.BlockSpec((gs, STATE2, ACOLS), lambda b, n, g: (g, 0, 0))],
        out_specs=[pl.BlockSpec((1, nk, CHUNK * D), lambda b, n, g: (b, n, 0)),
                   pl.BlockSpec((1, 1, gs, STATE2, 1), lambda b, n, g: (b, n, g, 0, 0))],
        out_shape=[jax.ShapeDtypeStruct((B, L // CHUNK, CHUNK * D), BF16),
                   jax.ShapeDtypeStruct((B, NB, G, STATE2, 1), F32)],
        scratch_shapes=[pltpu.VMEM((CHUNK, D, nk), BF16),
                        pltpu.VMEM((CHUNK, D, nk), BF16),
                        pltpu.VMEM((CHUNK, nk, 1), F32),
                        pltpu.VMEM((G, STATE2, 1), F32)],
        compiler_params=_cparams(("arbitrary", "arbitrary", "arbitrary")),
        name="s5_mix_prompt",
    )(*([x] * LANE_SLABS), pre_g.reshape(1, D), d_skip.reshape(1, D), wt, tm, acol)
    return z2, hout[:, NB - 1]


def _s5_mix_sample_kernel(x_ref, g_ref, d_ref, h0_ref, wt_ref, tm_ref, acol_ref, z_ref, hout_ref,
                          ut_ref, yt_ref, r_ref, *, nseq, ntok, gs):
    P, D = SSM_STATE, D_MODEL
    half = ntok * SSM_GROUP
    gg = pl.program_id(0)

    @pl.when(gg == 0)
    def _():
        for s in range(ntok):
            xs = x_ref[:, s * D:(s + 1) * D]
            r = _inv_rms(xs)
            r_ref[s] = r
            ut_ref[s] = (xs * r * g_ref[...]).T.astype(BF16)

    for gi in range(gs):
        g = gg * gs + gi
        row0 = pl.multiple_of(g * SSM_GROUP, SSM_GROUP)
        ug = ut_ref[:, pl.ds(row0, SSM_GROUP), :].reshape(half, nseq)
        h0 = h0_ref[gi]
        acol = acol_ref[gi]
        s_all = _dot(wt_ref[gi, :, half:], ug)
        er, ei = _cmul(acol[:P, ACOLS - 1:ACOLS], acol[P:, ACOLS - 1:ACOLS], h0[:P], h0[P:])
        hout_ref[gi] = jnp.concatenate([er + s_all[:P], ei + s_all[P:]], axis=0)
        y = _dot(tm_ref[gi, 0:half, 0:half], ug) + _dot(tm_ref[gi, 0:half, CHUNK_W:], h0.astype(BF16))
        yt_ref[:, pl.ds(row0, SSM_GROUP), :] = y.reshape(ntok, SSM_GROUP, nseq)

    @pl.when(gg == pl.num_programs(0) - 1)
    def _():
        for t in range(ntok):
            xs = x_ref[:, t * D:(t + 1) * D]
            u = xs * r_ref[t] * g_ref[...]
            v = yt_ref[t].T + d_ref[...] * u
            z_ref[:, t * D:(t + 1) * D] = _gelu(v).astype(BF16)


def _s5_mix_sample(x, h0, pre_g, d_skip, wt, tm, acol):
    S, T, D = x.shape
    assert T * 2 == CHUNK
    gs = 8
    G = N_GROUPS
    z2, hout = pl.pallas_call(
        functools.partial(_s5_mix_sample_kernel, nseq=S, ntok=T, gs=gs),
        grid=(G // gs,),
        in_specs=[pl.BlockSpec((S, T * D), lambda g: (0, 0)),
                  pl.BlockSpec((1, D), lambda g: (0, 0)),
                  pl.BlockSpec((1, D), lambda g: (0, 0)),
                  pl.BlockSpec((gs, STATE2, S), lambda g: (g, 0, 0)),
                  pl.BlockSpec((gs, STATE2, CHUNK_W), lambda g: (g, 0, 0)),
                  pl.BlockSpec((gs, CHUNK_W, CHUNK_W + STATE2), lambda g: (g, 0, 0)),
                  pl.BlockSpec((gs, STATE2, ACOLS), lambda g: (g, 0, 0))],
        out_specs=[pl.BlockSpec((S, T * D), lambda g: (0, 0)),
                   pl.BlockSpec((gs, STATE2, S), lambda g: (g, 0, 0))],
        out_shape=[jax.ShapeDtypeStruct((S, T * D), BF16),
                   jax.ShapeDtypeStruct((G, STATE2, S), F32)],
        scratch_shapes=[pltpu.VMEM((T, D, S), BF16),
                        pltpu.VMEM((T, D, S), F32),
                        pltpu.VMEM((T, S, 1), F32)],
        compiler_params=_cparams(("arbitrary",)),
        name="s5_mix_sample",
    )(x.reshape(S, T * D), pre_g.reshape(1, D), d_skip.reshape(1, D), h0, wt, tm, acol)
    return z2, hout


def _tail_kernel(a_ref, w_ref, g_ref, x_ref, o_ref, *, glu):
    y = _dot(a_ref[...], w_ref[...])
    if glu:
        n = y.shape[-1] // 2
        y = y[:, :n] * jax.nn.sigmoid(y[:, n:])
    o_ref[...] = x_ref[...] + y * _inv_rms(y) * g_ref[...]


def _tail(a2, w, g, x2, *, glu, tm):
    R = a2.shape[0]
    K = w.shape[0]
    D = D_MODEL
    n = a2.shape[1] // K
    tm = min(tm, R)
    assert R % tm == 0 and x2.shape == (R, n * D)
    return pl.pallas_call(
        functools.partial(_tail_kernel, glu=glu),
        grid=(R // tm, n),
        in_specs=[pl.BlockSpec((tm, K), lambda i, s: (i, s)),
                  _resident(w.shape),
                  pl.BlockSpec((1, D), lambda i, s: (0, 0)),
                  pl.BlockSpec((tm, D), lambda i, s: (i, s))],
        out_specs=pl.BlockSpec((tm, D), lambda i, s: (i, s)),
        out_shape=jax.ShapeDtypeStruct((R, n * D), F32),
        compiler_params=_cparams(("parallel", "parallel")),
        name="glu_tail" if glu else "oproj_tail",
    )(a2, w, g.reshape(1, D), x2)


def _glu_tail_prompt_kernel(z_ref, w_ref, g_ref, x_ref, o_ref, slab_ref, *, rk, ns):
    D = D_MODEL
    for s0 in range(0, CHUNK, ns):
        a = jnp.concatenate([z_ref[:, s * D:(s + 1) * D] for s in range(s0, s0 + ns)], axis=0)
        y = _dot(a, w_ref[...])
        y = y[:, :D] * jax.nn.sigmoid(y[:, D:])
        y = y * _inv_rms(y) * g_ref[...]
        for j in range(ns):
            for c in range(LANE_SLABS):
                slab_ref[c, pl.ds(s0 + j, rk, stride=CHUNK), :] = y[j * rk:(j + 1) * rk, c * LANES:(c + 1) * LANES]
    for c in range(LANE_SLABS):
        o_ref[:, c * LANES:(c + 1) * LANES] = x_ref[:, c * LANES:(c + 1) * LANES] + slab_ref[c]


def _glu_tail_prompt(z2, w, g, x2, *, rk, ns):
    R = z2.shape[0]
    D = D_MODEL
    assert R % rk == 0 and CHUNK % ns == 0 and x2.shape == (R * CHUNK, D)
    return pl.pallas_call(
        functools.partial(_glu_tail_prompt_kernel, rk=rk, ns=ns),
        grid=(R // rk,),
        in_specs=[pl.BlockSpec((rk, CHUNK * D), lambda i: (i, 0)),
                  _resident(w.shape),
                  pl.BlockSpec((1, D), lambda i: (0, 0)),
                  pl.BlockSpec((rk * CHUNK, D), lambda i: (i, 0))],
        out_specs=pl.BlockSpec((rk * CHUNK, D), lambda i: (i, 0)),
        out_shape=jax.ShapeDtypeStruct((R * CHUNK, D), F32),
        scratch_shapes=[pltpu.VMEM((LANE_SLABS, rk * CHUNK, LANES), F32)],
        compiler_params=_cparams(("parallel",)),
        name="glu_tail_prompt",
    )(z2, w, g.reshape(1, D), x2)


def _kvq_kernel(x_ref, gkv_ref, gq_ref, wk_ref, wv_ref, wq_ref, k_ref, v_ref, *q_refs, by_head, q_scale):
    x = x_ref[...]
    xr = x * _inv_rms(x)
    kv_in = (xr * gkv_ref[...]).astype(BF16)
    xn = (xr * gq_ref[...]).astype(BF16)
    k = _dot(kv_in, wk_ref[...])
    v = _dot(kv_in, wv_ref[...])
    q = _dot(xn, wq_ref[...]) * q_scale
    if by_head:
        for ref, val in ((k_ref, k), (v_ref, v), (q_refs[0], q)):
            for h in range(N_HEADS):
                ref[:, h, :] = val[:, h * V_DIM:(h + 1) * V_DIM]
    else:
        kb_ref, vb_ref, qb_ref = q_refs
        k_ref[...] = k
        v_ref[...] = v
        kb_ref[...] = k.astype(BF16)
        vb_ref[...] = v.astype(BF16)
        qb_ref[...] = q.astype(BF16)


def _kvq(x2, g_kv, g_q, wk, wv, wq, *, tm, by_head, q_scale):
    R, D = x2.shape
    tm = min(tm, R)
    assert R % tm == 0
    tile = pl.BlockSpec((tm, D), lambda i: (i, 0))
    heads = pl.BlockSpec((tm, N_HEADS, V_DIM), lambda i: (i, 0, 0))
    vec = pl.BlockSpec((1, D), lambda i: (0, 0))
    head32 = jax.ShapeDtypeStruct((R, N_HEADS, V_DIM), F32)
    flat32 = jax.ShapeDtypeStruct((R, D), F32)
    flat16 = jax.ShapeDtypeStruct((R, D), BF16)
    return pl.pallas_call(
        functools.partial(_kvq_kernel, by_head=by_head, q_scale=q_scale),
        grid=(R // tm,),
        in_specs=[tile, vec, vec, _resident(wk.shape), _resident(wv.shape), _resident(wq.shape)],
        out_specs=[heads] * 3 if by_head else [tile] * 5,
        out_shape=[head32] * 3 if by_head else [flat32, flat32, flat16, flat16, flat16],
        compiler_params=_cparams(("parallel",)),
        name="kvq_proj",
    )(x2, g_kv.reshape(1, D), g_q.reshape(1, D), wk, wv, wq)


FF_CHUNK = D_FF // 2


def _ffn_cols(c):
    return (slice(c * FF_CHUNK, (c + 1) * FF_CHUNK),
            slice(D_FF + c * FF_CHUNK, D_FF + (c + 1) * FF_CHUNK))


def _ffn_prompt_kernel(x_ref, gpre_ref, gpost_ref, wup_ref, cw_ref, cb_ref, wdn_ref,
                       o_ref, cs_ref, carry_ref, *, tm):
    @pl.when(pl.program_id(1) == 0)
    def _():
        carry_ref[...] = jnp.zeros_like(carry_ref)

    x = x_ref[0]
    xn = (x * _inv_rms(x) * gpre_ref[...]).astype(BF16)
    rows = lax.broadcasted_iota(jnp.int32, (tm, 1), 0)
    f = jnp.zeros((tm, D_MODEL), F32)
    for c in range(D_FF // FF_CHUNK):
        halves = []
        for cols in _ffn_cols(c):
            h = _dot(xn, wup_ref[:, cols])
            c0, c1 = carry_ref[6:7, cols], carry_ref[7:8, cols]
            h1 = jnp.where(rows == 0, c1, pltpu.roll(h, 1, axis=0))
            h2 = jnp.where(rows == 0, c0, jnp.where(rows == 1, c1, pltpu.roll(h, 2, axis=0)))
            halves.append(cb_ref[:, cols] + cw_ref[2:3, cols] * h
                          + cw_ref[1:2, cols] * h1 + cw_ref[0:1, cols] * h2)
            carry_ref[:, cols] = h[tm - 8:tm]
            cs_ref[0, :, cols] = h[tm - 2:tm]
        a = (_gelu(halves[0]) * halves[1]).astype(BF16)
        f = f + _dot(a, wdn_ref[c * FF_CHUNK:(c + 1) * FF_CHUNK, :])
    o_ref[0] = x + f * _inv_rms(f) * gpost_ref[...]


def _ffn_prompt(x, gpre, gpost, wup, cw, cb, wdn, *, tm):
    B, L, D = x.shape
    tm = min(tm, L)
    assert L % tm == 0
    F2 = 2 * D_FF
    vec = pl.BlockSpec((1, D), lambda b, t: (0, 0))
    return pl.pallas_call(
        functools.partial(_ffn_prompt_kernel, tm=tm),
        grid=(B, L // tm),
        in_specs=[pl.BlockSpec((1, tm, D), lambda b, t: (b, t, 0)), vec, vec,
                  _resident(wup.shape), _resident((CONV_W, F2)), _resident((1, F2)),
                  _resident(wdn.shape)],
        out_specs=[pl.BlockSpec((1, tm, D), lambda b, t: (b, t, 0)),
                   pl.BlockSpec((1, CONV_W - 1, F2), lambda b, t: (b, 0, 0))],
        out_shape=[jax.ShapeDtypeStruct((B, L, D), F32),
                   jax.ShapeDtypeStruct((B, CONV_W - 1, F2), F32)],
        scratch_shapes=[pltpu.VMEM((8, F2), F32)],
        compiler_params=_cparams(("arbitrary", "arbitrary")),
        name="ffn_prompt",
    )(x, gpre.reshape(1, D), gpost.reshape(1, D), wup, cw, cb.reshape(1, F2), wdn)


def _ffn_sample_kernel(x_ref, buf_ref, gpre_ref, gpost_ref, wup_ref, cw_ref, cb_ref, wdn_ref,
                       o_ref, cs_ref, *, tm, ntok):
    nsq = tm // ntok
    x = x_ref[...]
    xn = (x * _inv_rms(x) * gpre_ref[...]).astype(BF16)
    sub = lax.broadcasted_iota(jnp.int32, (tm, 1), 0) % ntok
    f = jnp.zeros((tm, D_MODEL), F32)
    for c in range(D_FF // FF_CHUNK):
        halves = []
        for cols in _ffn_cols(c):
            h = _dot(xn, wup_ref[:, cols])
            spread = lambda b: jnp.broadcast_to(b, (nsq, ntok, FF_CHUNK)).reshape(tm, FF_CHUNK)
            b0, b1 = spread(buf_ref[:, 0:1, cols]), spread(buf_ref[:, 1:2, cols])
            h1 = jnp.where(sub == 0, b1, pltpu.roll(h, 1, axis=0))
            h2 = jnp.where(sub == 0, b0, jnp.where(sub == 1, b1, pltpu.roll(h, 2, axis=0)))
            halves.append(cb_ref[:, cols] + cw_ref[2:3, cols] * h
                          + cw_ref[1:2, cols] * h1 + cw_ref[0:1, cols] * h2)
            cs_ref[:, :, cols] = h.reshape(nsq, ntok, FF_CHUNK)[:, ntok - 2:ntok, :]
        a = (_gelu(halves[0]) * halves[1]).astype(BF16)
        f = f + _dot(a, wdn_ref[c * FF_CHUNK:(c + 1) * FF_CHUNK, :])
    o_ref[...] = x + f * _inv_rms(f) * gpost_ref[...]


def _ffn_sample(x, buf, gpre, gpost, wup, cw, cb, wdn, *, tm):
    S, T, D = x.shape
    assert T == 8 and tm % T == 0 and (S * T) % tm == 0
    F2 = 2 * D_FF
    nsq = tm // T
    vec = pl.BlockSpec((1, D), lambda i: (0, 0))
    out, cs = pl.pallas_call(
        functools.partial(_ffn_sample_kernel, tm=tm, ntok=T),
        grid=(S * T // tm,),
        in_specs=[pl.BlockSpec((tm, D), lambda i: (i, 0)),
                  pl.BlockSpec((nsq, CONV_W - 1, F2), lambda i: (i, 0, 0)), vec, vec,
                  _resident(wup.shape), _resident((CONV_W, F2)), _resident((1, F2)),
                  _resident(wdn.shape)],
        out_specs=[pl.BlockSpec((tm, D), lambda i: (i, 0)),
                   pl.BlockSpec((nsq, CONV_W - 1, F2), lambda i: (i, 0, 0))],
        out_shape=[jax.ShapeDtypeStruct((S * T, D), F32),
                   jax.ShapeDtypeStruct((S, CONV_W - 1, F2), F32)],
        compiler_params=_cparams(("parallel",)),
        name="ffn_sample",
    )(x.reshape(S * T, D), buf, gpre.reshape(1, D), gpost.reshape(1, D), wup, cw,
      cb.reshape(1, F2), wdn)
    return out.reshape(S, T, D), cs


def _lam_value(lq1, lk1, lq2, lk2, lam_init):
    return (jnp.exp(jnp.sum(lq1[...] * lk1[...], axis=-1, keepdims=True))
            - jnp.exp(jnp.sum(lq2[...] * lk2[...], axis=-1, keepdims=True)) + lam_init)


def _sub_norm(o, subg, lam_init):
    return o * _inv_rms(o) * subg * (1.0 - lam_init)


LOG2E = math.log2(math.e)
FLASH_ROWS = 256
POS_SPLIT = 64


def _flash_kernel(qi_ref, ki_ref, q_ref, k_ref, v_ref, lq1, lk1, lq2, lk2, subg_ref, o_ref,
                  q2_ref, kf_ref, m_ref, acc_ref, *, tq, lam_init):
    h, step = pl.program_id(1), pl.program_id(2)
    qi, ki = qi_ref[step], ki_ref[step]
    RQ = FLASH_ROWS
    n_chunks = tq // RQ
    lane = lax.broadcasted_iota(jnp.int32, (1, V_DIM), 1)
    slope2 = jnp.exp2(-jnp.full((1, V_DIM), h + 1, jnp.int32).astype(F32)) * LOG2E

    @pl.when(ki == 0)
    def _():
        m_ref[...] = jnp.full_like(m_ref, -jnp.inf)
        acc_ref[...] = jnp.zeros_like(acc_ref)
        kidx = lax.broadcasted_iota(jnp.int32, (tq, 1), 0)
        a, b = (kidx // POS_SPLIT).astype(F32), (kidx % POS_SPLIT).astype(F32)
        kf_ref[...] = jnp.where(lane < 3, a, jnp.where(lane < 6, b, 0.0)).astype(BF16)
        c0 = slope2.astype(BF16).astype(F32)
        c1 = (slope2 - c0).astype(BF16).astype(F32)
        c2 = (slope2 - c0 - c1).astype(BF16).astype(F32)
        part = jnp.where(lane % 3 == 0, c0, jnp.where(lane % 3 == 1, c1, c2))
        qf = jnp.where(lane < 3, POS_SPLIT * part, jnp.where(lane < 6, part, 0.0)).astype(BF16)
        qf = jnp.broadcast_to(qf, (RQ, V_DIM))
        for c_i in range(n_chunks):
            q = q_ref[c_i * RQ:(c_i + 1) * RQ, :]
            zero = jnp.zeros_like(q)
            q2_ref[c_i, 0:RQ, 0:V_DIM] = jnp.where(lane < HEAD_DIM, q, zero)
            q2_ref[c_i, RQ:2 * RQ, 0:V_DIM] = jnp.where(lane >= HEAD_DIM, q, zero)
            q2_ref[c_i, 0:RQ, V_DIM:] = qf
            q2_ref[c_i, RQ:2 * RQ, V_DIM:] = qf

    def absorb(diagonal):
        k_aug = jnp.concatenate([k_ref[...], kf_ref[...]], axis=1)
        v_aug = jnp.concatenate([v_ref[...], jnp.ones((tq, V_DIM), BF16)], axis=1)
        shift = slope2 * ((ki - qi) * tq).astype(F32)
        for c_i in range(n_chunks):
            nkeys = (c_i + 1) * RQ if diagonal else tq
            s = lax.dot_general(q2_ref[c_i], k_aug[0:nkeys], (((1,), (1,)), ((), ())),
                                preferred_element_type=F32)
            if diagonal:
                kcol = lax.broadcasted_iota(jnp.int32, (1, RQ), 1)
                qrow = lax.broadcasted_iota(jnp.int32, (2 * RQ, 1), 0) % RQ
                tail = jnp.where(kcol <= qrow, s[:, nkeys - RQ:], -jnp.inf)
                s = tail if nkeys == RQ else jnp.concatenate([s[:, 0:nkeys - RQ], tail], axis=1)
            m_old = m_ref[c_i]
            m_new = jnp.maximum(m_old, jnp.max(s, axis=-1, keepdims=True) + shift)
            alpha = jnp.exp2(m_old - m_new)
            p = jnp.exp2(s - jnp.concatenate([m_new - shift] * (nkeys // V_DIM), axis=1))
            pv = _dot(p.astype(BF16), v_aug[0:nkeys])
            acc_ref[c_i] = jnp.concatenate([alpha, alpha], axis=1) * acc_ref[c_i] + pv
            m_ref[c_i] = m_new

    @pl.when(ki < qi)
    def _():
        absorb(False)

    @pl.when(ki == qi)
    def _():
        absorb(True)
        lam = _lam_value(lq1, lk1, lq2, lk2, lam_init)
        for c_i in range(n_chunks):
            acc = acc_ref[c_i]
            o = (acc[0:RQ, 0:V_DIM] / acc[0:RQ, V_DIM:]
                 - lam * (acc[RQ:, 0:V_DIM] / acc[RQ:, V_DIM:]))
            o_ref[c_i * RQ:(c_i + 1) * RQ, :] = _sub_norm(o, subg_ref[...], lam_init).astype(o_ref.dtype)


def _flash_prompt(qb, kb, vb, lq1, lk1, lq2, lk2, subg, *, B, L, tq, lam_init):
    tq = min(tq, L)
    assert L % tq == 0 and tq % FLASH_ROWS == 0 and tq <= POS_SPLIT * POS_SPLIT
    nq = L // tq
    n_chunks = tq // FLASH_ROWS
    pairs = [(i, j) for i in range(nq) for j in range(i + 1)]
    qi_tab = jnp.asarray([p[0] for p in pairs], jnp.int32)
    ki_tab = jnp.asarray([p[1] for p in pairs], jnp.int32)
    lvec = pl.BlockSpec((1, HEAD_DIM), lambda b, h, s, qt, kt: (0, 0))
    grid_spec = pltpu.PrefetchScalarGridSpec(
        num_scalar_prefetch=2,
        grid=(B, N_HEADS, len(pairs)),
        in_specs=[pl.BlockSpec((tq, V_DIM), lambda b, h, s, qt, kt: (b * nq + qt[s], h)),
                  pl.BlockSpec((tq, V_DIM), lambda b, h, s, qt, kt: (b * nq + kt[s], h)),
                  pl.BlockSpec((tq, V_DIM), lambda b, h, s, qt, kt: (b * nq + kt[s], h)),
                  lvec, lvec, lvec, lvec,
                  pl.BlockSpec((1, V_DIM), lambda b, h, s, qt, kt: (0, 0))],
        out_specs=pl.BlockSpec((tq, V_DIM), lambda b, h, s, qt, kt: (b * nq + qt[s], h)),
        scratch_shapes=[pltpu.VMEM((n_chunks, 2 * FLASH_ROWS, 2 * V_DIM), BF16),
                        pltpu.VMEM((tq, V_DIM), BF16),
                        pltpu.VMEM((n_chunks, 2 * FLASH_ROWS, V_DIM), F32),
                        pltpu.VMEM((n_chunks, 2 * FLASH_ROWS, 2 * V_DIM), F32)])
    return pl.pallas_call(
        functools.partial(_flash_kernel, tq=tq, lam_init=lam_init),
        grid_spec=grid_spec,
        out_shape=jax.ShapeDtypeStruct((B * L, N_HEADS * V_DIM), BF16),
        compiler_params=_cparams(("parallel", "parallel", "arbitrary")),
        name="flash_prompt",
    )(qi_tab, ki_tab, qb, kb, vb, lq1, lk1, lq2, lk2, subg)


def _paged_kernel(pt_ref, q_ref, *refs, ntok, page, npp, past_len, lam_init):
    kc = refs[:npp]
    vc = refs[npp:2 * npp]
    (kn_ref, vn_ref, lq1, lk1, lq2, lk2, subg_ref, o_ref,
     qall_ref, bias_ref, m_ref, l_ref, acc_ref) = refs[2 * npp:]
    H = N_HEADS
    R = 2 * ntok * H
    PW = page * H
    pg = pl.program_id(1)

    row = lax.broadcasted_iota(jnp.int32, (R, 1), 0)
    slope_r = jnp.exp2(-(row % H + 1).astype(F32)) * LOG2E
    qi_r = (row // H) % ntok

    def head_bias(n_lanes, key_limit):
        lane = lax.broadcasted_iota(jnp.int32, (1, n_lanes), 1)
        key = lane // H
        ok = (lane % H == row % H) & (key < key_limit)
        return jnp.where(ok, slope_r * (key - qi_r).astype(F32), -jnp.inf)

    @pl.when(pg == 0)
    def _():
        m_ref[...] = jnp.full_like(m_ref, -jnp.inf)
        l_ref[...] = jnp.zeros_like(l_ref)
        acc_ref[...] = jnp.zeros_like(acc_ref)
        q2 = q_ref[...].reshape(ntok * H, V_DIM)
        col = lax.broadcasted_iota(jnp.int32, (1, V_DIM), 1)
        qall_ref[...] = jnp.concatenate([jnp.where(col < HEAD_DIM, q2, 0.0),
                                         jnp.where(col >= HEAD_DIM, q2, 0.0)], axis=0).astype(BF16)
        bias_ref[...] = head_bias(PW, page)

    def absorb(ks, vs, bias, shifts):
        qall = qall_ref[...]
        s = [lax.dot_general(qall, k, (((1,), (1,)), ((), ())), preferred_element_type=F32) + bias
             for k in ks]
        m_old = m_ref[...]
        m_new = m_old
        for si, sh in zip(s, shifts):
            m_new = jnp.maximum(m_new, jnp.max(si, axis=-1, keepdims=True) + sh)
        alpha = jnp.exp2(m_old - m_new)
        l_new = alpha * l_ref[...]
        acc = alpha * acc_ref[...]
        for si, sh, v in zip(s, shifts, vs):
            p = jnp.exp2(si - (m_new - sh))
            l_new = l_new + jnp.sum(p, axis=-1, keepdims=True)
            acc = acc + _dot(p.astype(BF16), v)
        m_ref[...] = m_new
        l_ref[...] = l_new
        acc_ref[...] = acc

    flat = lambda ref: ref[0].reshape(PW, V_DIM).astype(BF16)
    shifts = [slope_r * ((pg * npp + i) * page - past_len).astype(F32) for i in range(npp)]
    absorb([flat(r) for r in kc], [flat(r) for r in vc], bias_ref[...], shifts)

    @pl.when(pg == pl.num_programs(1) - 1)
    def _():
        n_new = ntok * H
        pad = jnp.zeros((R - n_new, V_DIM), F32)
        kn = jnp.concatenate([kn_ref[...].reshape(n_new, V_DIM), pad], axis=0).astype(BF16)
        vn = jnp.concatenate([vn_ref[...].reshape(n_new, V_DIM), pad], axis=0).astype(BF16)
        absorb([kn], [vn], head_bias(R, jnp.minimum(qi_r + 1, ntok)), [0.0])
        lam = _lam_value(lq1, lk1, lq2, lk2, lam_init)
        h = R // 2
        o = acc_ref[0:h] / l_ref[0:h] - lam * (acc_ref[h:R] / l_ref[h:R])
        o_ref[...] = _sub_norm(o, subg_ref[...], lam_init).reshape(ntok, H, V_DIM)


def _paged_sample(page_table, q, cache_k, cache_v, k_new, v_new, lq1, lk1, lq2, lk2, subg,
                  *, ntok, npp, lam_init):
    S, n_pages = page_table.shape
    page = cache_k.shape[1]
    H = N_HEADS
    R = 2 * ntok * H
    assert n_pages % npp == 0
    tok = pl.BlockSpec((ntok, H, V_DIM), lambda s, p, pt: (s, 0, 0))
    pgs = [pl.BlockSpec((1, page, H, V_DIM), functools.partial(
        lambda s, p, pt, i: (pt[s, p * npp + i], 0, 0, 0), i=i)) for i in range(npp)]
    lvec = pl.BlockSpec((1, HEAD_DIM), lambda s, p, pt: (0, 0))
    grid_spec = pltpu.PrefetchScalarGridSpec(
        num_scalar_prefetch=1,
        grid=(S, n_pages // npp),
        in_specs=[tok] + pgs + pgs + [tok, tok, lvec, lvec, lvec, lvec,
                                      pl.BlockSpec((1, V_DIM), lambda s, p, pt: (0, 0))],
        out_specs=tok,
        scratch_shapes=[pltpu.VMEM((R, V_DIM), BF16), pltpu.VMEM((R, page * H), F32),
                        pltpu.VMEM((R, 1), F32), pltpu.VMEM((R, 1), F32), pltpu.VMEM((R, V_DIM), F32)])
    return pl.pallas_call(
        functools.partial(_paged_kernel, ntok=ntok, page=page, npp=npp, past_len=n_pages * page,
                          lam_init=lam_init),
        grid_spec=grid_spec,
        out_shape=jax.ShapeDtypeStruct((S * ntok, H, V_DIM), F32),
        compiler_params=_cparams(("arbitrary", "arbitrary")),
        name="paged_sample",
    )(page_table, q, *([cache_k] * npp), *([cache_v] * npp), k_new, v_new, lq1, lk1, lq2, lk2, subg)


def kernel(x_prompt, x_sample, state_ssm_re, state_ssm_im, state_conv, cache_k, cache_v, page_table,
           a_pre_g, a_post_g, ssm_lam_re, ssm_lam_im, ssm_log_dt, ssm_b_re, ssm_b_im, ssm_c_re,
           ssm_c_im, ssm_d, glu_w, kv_norm_g, w_k, w_v, b_pre_g, b_post_g, w_q, lam_q1, lam_k1,
           lam_q2, lam_k2, sub_g, w_o, f_pre_g, f_post_g, w_up, conv_w, conv_b, w_down):
    B, L, D = x_prompt.shape
    S, T, _ = x_sample.shape
    P, G = SSM_STATE, N_GROUPS
    HW = N_HEADS * V_DIM
    st_dtype = state_ssm_re.dtype

    wt, tm_op, acol = _s5_prepare(ssm_lam_re[0], ssm_lam_im[0], ssm_log_dt[0], ssm_b_re[0],
                                  ssm_b_im[0], ssm_c_re[0], ssm_c_im[0])
    glu_b = glu_w[0].astype(BF16)
    zp, hp = _s5_mix_prompt(x_prompt, a_pre_g[0], ssm_d[0], wt, tm_op, acol)
    xp = _glu_tail_prompt(zp.reshape(B * L // CHUNK, CHUNK * D), glu_b, a_post_g[0],
                          x_prompt.reshape(B * L, D), rk=64, ns=8).reshape(B, L, D)
    h0 = jnp.concatenate([state_ssm_re[0].astype(F32), state_ssm_im[0].astype(F32)], axis=-1)
    zs, hs = _s5_mix_sample(x_sample, h0.transpose(1, 2, 0), a_pre_g[0], ssm_d[0], wt, tm_op, acol)
    xs = _tail(zs, glu_b, a_post_g[0], x_sample.reshape(S, T * D), glu=True, tm=512).reshape(S, T, D)

    hp = hp.reshape(B, G, 2 * P)
    hs = hs.transpose(2, 0, 1)
    ssm_re_p, ssm_im_p = hp[None, ..., :P].astype(st_dtype), hp[None, ..., P:].astype(st_dtype)
    ssm_re_s, ssm_im_s = hs[None, ..., :P].astype(st_dtype), hs[None, ..., P:].astype(st_dtype)

    wup_b, wdn_b = w_up.astype(BF16), w_down.astype(BF16)
    xp, conv_p0 = _ffn_prompt(xp, f_pre_g[0], f_post_g[0], wup_b[0], conv_w[0], conv_b[0], wdn_b[0], tm=512)
    xs, conv_s0 = _ffn_sample(xs, state_conv[0], f_pre_g[0], f_post_g[0], wup_b[0], conv_w[0],
                              conv_b[0], wdn_b[0], tm=256)

    lam_init = 0.8 - 0.6 * math.exp(-0.3 * N_A_LAYERS)
    wk_b, wv_b, wq_b, wo_b = w_k.astype(BF16), w_v.astype(BF16), w_q[0].astype(BF16), w_o[0].astype(BF16)
    lvec = lambda a: a[0].reshape(1, HEAD_DIM).astype(F32)
    lams = (lvec(lam_q1), lvec(lam_k1), lvec(lam_q2), lvec(lam_k2))
    subg = sub_g[0].reshape(1, V_DIM)

    xp2 = xp.reshape(B * L, D)
    kp, vp, kpb, vpb, qpb = _kvq(xp2, kv_norm_g, b_pre_g[0], wk_b, wv_b, wq_b, tm=512, by_head=False,
                                 q_scale=LOG2E * HEAD_DIM ** -0.5)
    op = _flash_prompt(qpb, kpb, vpb, *lams, subg, B=B, L=L, tq=2048, lam_init=lam_init)
    xp = _tail(op, wo_b, b_post_g[0], xp2, glu=False, tm=512).reshape(B, L, D)

    xs2 = xs.reshape(S * T, D)
    ks, vs, qs = _kvq(xs2, kv_norm_g, b_pre_g[0], wk_b, wv_b, wq_b, tm=512, by_head=True,
                      q_scale=LOG2E * HEAD_DIM ** -0.5)
    os_ = _paged_sample(page_table, qs, cache_k, cache_v, ks, vs, *lams, subg, ntok=T, npp=16,
                        lam_init=lam_init)
    xs = _tail(os_.reshape(S * T, HW).astype(BF16), wo_b, b_post_g[0], xs2, glu=False, tm=512).reshape(S, T, D)

    xp, conv_p1 = _ffn_prompt(xp, f_pre_g[1], f_post_g[1], wup_b[1], conv_w[1], conv_b[1], wdn_b[1], tm=512)
    xs, conv_s1 = _ffn_sample(xs, state_conv[1], f_pre_g[1], f_post_g[1], wup_b[1], conv_w[1],
                              conv_b[1], wdn_b[1], tm=256)

    return (xp, xs, ssm_re_p, ssm_im_p, jnp.stack([conv_p0, conv_p1]),
            kp.reshape(B, L, N_HEADS, 2 * HEAD_DIM), vp.reshape(B, L, N_HEADS, V_DIM),
            ssm_re_s, ssm_im_s, jnp.stack([conv_s0, conv_s1]),
            ks.reshape(S, T, N_HEADS, 2 * HEAD_DIM), vs.reshape(S, T, N_HEADS, V_DIM))
```

```python
import functools
import math

import jax
import jax.numpy as jnp
from jax import lax
from jax.experimental import pallas as pl
from jax.experimental.pallas import tpu as pltpu

D_MODEL = 1024
SSM_GROUP = 16
N_GROUPS = D_MODEL // SSM_GROUP
SSM_STATE = 64
N_HEADS = 8
HEAD_DIM = 64
V_DIM = 2 * HEAD_DIM
D_FF = 2816
CONV_W = 3
NORM_EPS = 1e-6
N_A_LAYERS = 1

LANES = 128
LANE_SLABS = D_MODEL // LANES
CHUNK = 16
STATE2 = 2 * SSM_STATE
CHUNK_W = CHUNK * SSM_GROUP
SCAN_STEPS = 7
ACOLS = 8
PREP_GROUPS = 16
HIGHEST = lax.Precision.HIGHEST
BF16 = jnp.bfloat16
F32 = jnp.float32
VMEM_LIMIT = 56 * 1024 * 1024


def _cparams(sem, vmem=VMEM_LIMIT):
    return pltpu.CompilerParams(dimension_semantics=sem, vmem_limit_bytes=vmem)


def _resident(shape):
    zeros = (0,) * len(shape)
    return pl.BlockSpec(shape, lambda *_: zeros, pipeline_mode=pl.Buffered(1))


def _inv_rms(x):
    return lax.rsqrt(jnp.mean(x * x, axis=-1, keepdims=True) + NORM_EPS)


def _gelu(x):
    c = math.sqrt(2.0 / math.pi)
    return x * (0.5 * (1.0 + jnp.tanh(c * (x + 0.044715 * (x * x * x)))))


def _dot(a, b):
    return jnp.dot(a, b, preferred_element_type=F32)


def _s5_prep_kernel(lre, lim, ldt, bre, bim, cre, cim, wt_ref, tm_ref, acol_ref, w_scr, z_scr):
    P, C, GB = SSM_STATE, SSM_GROUP, PREP_GROUPS
    dt = jnp.exp(ldt[...])
    lr, li = lre[...], lim[...]
    ar, ai = lr * dt, li * dt
    first = lax.broadcasted_iota(jnp.int32, (1, 2 * P), 1) < P
    sgn = jnp.where(first, -1.0, 1.0)

    def cpow(n):
        m = jnp.exp(n * ar)
        pr, pi = m * jnp.cos(n * ai), m * jnp.sin(n * ai)
        return jnp.where(first, pr, pi), jnp.where(first, pi, pr)

    def cmul(x1, x2, yr, yi):
        return x1[:, None, :] * yr + (sgn * x2)[:, None, :] * yi

    lbr, lbi = jnp.exp(ar) * jnp.cos(ai), jnp.exp(ar) * jnp.sin(ai)
    den = lr * lr + li * li
    nr, ni = lbr - 1.0, lbi
    cr = (nr * lr + ni * li) / den
    ci = (ni * lr - nr * li) / den
    bbr = cr[:, None, :] * bre[...] - ci[:, None, :] * bim[...]
    bbi = cr[:, None, :] * bim[...] + ci[:, None, :] * bre[...]
    for s in range(CHUNK):
        w_scr[:, s * C:(s + 1) * C, :] = cmul(*cpow(float(CHUNK - 1 - s)), bbr, bbi)
    for i in range(ACOLS):
        n = float(CHUNK * 2 ** i) if i < SCAN_STEPS else float(CHUNK // 2)
        acol_ref[i] = cpow(n)[0]
    for m in range(CHUNK + 1):
        z_scr[:, m * C:(m + 1) * C, :] = -sgn * cmul(*cpow(float(m)), cre[...], cim[...])
    for g in range(GB):
        wt_ref[g] = w_scr[g].T.astype(BF16)
    bb = jnp.where(first, bbr, bbi)
    bb_rep = jnp.concatenate([bb] * CHUNK, axis=1)
    kw = lax.dot_general(z_scr[:, 0:CHUNK_W, :], bb_rep, (((2,), (2,)), ((0,), (0,))), precision=HIGHEST,
                         preferred_element_type=F32)
    blk = lax.broadcasted_iota(jnp.int32, (1, 1, CHUNK_W), 2) // C
    toe = jnp.where(blk == 0, kw, 0.0)
    for s in range(1, CHUNK):
        down = jnp.concatenate([jnp.zeros((GB, s * C, CHUNK_W), F32), kw[:, 0:CHUNK_W - s * C, :]], axis=1)
        toe = jnp.where(blk == s, down, toe)
    tm_ref[:, :, 0:CHUNK_W] = toe.astype(BF16)
    tm_ref[:, :, CHUNK_W:] = z_scr[:, C:, :].astype(BF16)


def _s5_prepare(lam_re, lam_im, log_dt, b_re, b_im, c_re, c_im):
    G, P, C, GB = N_GROUPS, SSM_STATE, SSM_GROUP, PREP_GROUPS
    blk = lambda *shape: pl.BlockSpec((GB,) + shape, lambda i: (i,) + (0,) * len(shape))
    dup = lambda a: jnp.concatenate([a, a], axis=-1)
    wt, tm, acol = pl.pallas_call(
        _s5_prep_kernel,
        grid=(G // GB,),
        in_specs=[blk(2 * P), blk(2 * P), blk(1), blk(C, 2 * P), blk(C, 2 * P), blk(C, 2 * P), blk(C, 2 * P)],
        out_specs=[blk(STATE2, CHUNK_W), blk(CHUNK_W, CHUNK_W + STATE2),
                   pl.BlockSpec((ACOLS, GB, 2 * P), lambda i: (0, i, 0))],
        out_shape=[jax.ShapeDtypeStruct((G, STATE2, CHUNK_W), BF16),
                   jax.ShapeDtypeStruct((G, CHUNK_W, CHUNK_W + STATE2), BF16),
                   jax.ShapeDtypeStruct((ACOLS, G, 2 * P), F32)],
        scratch_shapes=[pltpu.VMEM((GB, CHUNK_W, 2 * P), F32), pltpu.VMEM((GB, CHUNK_W + C, 2 * P), F32)],
        compiler_params=_cparams(("parallel",)),
        name="s5_prep",
    )(dup(lam_re), dup(lam_im), log_dt.reshape(G, 1), dup(b_re.transpose(0, 2, 1)),
      dup(b_im.transpose(0, 2, 1)), dup(c_re), dup(c_im))
    return wt, tm, acol.transpose(1, 2, 0)


def _cmul(ar, ai, br, bi):
    return ar * br - ai * bi, ar * bi + ai * br


def _s5_mix_prompt_kernel(*refs, nk, gs):
    x_refs = refs[:LANE_SLABS]
    (g_ref, d_ref, wt_ref, tm_ref, acol_ref, z_ref, hout_ref,
     ut_ref, yt_ref, r_ref, carry_ref) = refs[LANE_SLABS:]
    P, D = SSM_STATE, D_MODEL
    nb, gg = pl.program_id(1), pl.program_id(2)

    def slot(s):
        return jnp.concatenate([x[0, pl.ds(s, nk, stride=CHUNK), :] for x in x_refs], axis=1)

    @pl.when(gg == 0)
    def _():
        @pl.when(nb == 0)
        def _():
            carry_ref[...] = jnp.zeros_like(carry_ref)
        for s in range(CHUNK):
            xs = slot(s)
            r = _inv_rms(xs)
            r_ref[s] = r
            ut_ref[s] = (xs * r * g_ref[...]).T.astype(BF16)

    lane = lax.broadcasted_iota(jnp.int32, (gs * P, nk), 1)
    row0 = pl.multiple_of(gg * (gs * SSM_GROUP), gs * SSM_GROUP)
    u_all = ut_ref[:, pl.ds(row0, gs * SSM_GROUP), :]
    ugs = [u_all[:, gi * SSM_GROUP:(gi + 1) * SSM_GROUP, :].reshape(CHUNK_W, nk) for gi in range(gs)]
    s_loc = [_dot(wt_ref[gi], ugs[gi]) for gi in range(gs)]
    stack = lambda parts: jnp.concatenate(parts, axis=0)
    hin = carry_ref[pl.ds(gg * gs, gs)]
    hin_r, hin_i = hin[:, :P].reshape(gs * P, 1), hin[:, P:].reshape(gs * P, 1)
    acol = acol_ref[...]
    a_r, a_i = acol[:, :P].reshape(gs * P, ACOLS), acol[:, P:].reshape(gs * P, ACOLS)
    cr, ci = _cmul(a_r[:, 0:1], a_i[:, 0:1], hin_r, hin_i)
    sr = stack([s[:P] for s in s_loc]) + jnp.where(lane == 0, cr, 0.0)
    si = stack([s[P:] for s in s_loc]) + jnp.where(lane == 0, ci, 0.0)
    for i in range(SCAN_STEPS):
        sh = 1 << i
        pr = jnp.where(lane >= sh, pltpu.roll(sr, sh, axis=1), 0.0)
        pi = jnp.where(lane >= sh, pltpu.roll(si, sh, axis=1), 0.0)
        qr, qi = _cmul(a_r[:, i:i + 1], a_i[:, i:i + 1], pr, pi)
        sr, si = sr + qr, si + qi
    hpr = jnp.where(lane >= 1, pltpu.roll(sr, 1, axis=1), hin_r)
    hpi = jnp.where(lane >= 1, pltpu.roll(si, 1, axis=1), hin_i)
    end_r, end_i = sr[:, nk - 1:nk].reshape(gs, P, 1), si[:, nk - 1:nk].reshape(gs, P, 1)
    hend = jnp.concatenate([end_r, end_i], axis=1)
    carry_ref[pl.ds(gg * gs, gs)] = hend
    hout_ref[0, 0] = hend
    ys = []
    for gi in range(gs):
        rows = slice(gi * P, (gi + 1) * P)
        hprev = stack([hpr[rows], hpi[rows]]).astype(BF16)
        y = _dot(tm_ref[gi, :, 0:CHUNK_W], ugs[gi]) + _dot(tm_ref[gi, :, CHUNK_W:], hprev)
        ys.append(y.reshape(CHUNK, SSM_GROUP, nk).astype(BF16))
    yt_ref[:, pl.ds(row0, gs * SSM_GROUP), :] = jnp.concatenate(ys, axis=1)

    @pl.when(gg == pl.num_programs(2) - 1)
    def _():
        for t in range(CHUNK):
            u = slot(t) * r_ref[t] * g_ref[...]
            v = yt_ref[t].astype(F32).T + d_ref[...] * u
            z_ref[0, :, t * D:(t + 1) * D] = _gelu(v).astype(BF16)


def _s5_mix_prompt(x, pre_g, d_skip, wt, tm, acol):
    B, L, D = x.shape
    nk = 1 << SCAN_STEPS
    blk = nk * CHUNK
    assert L % blk == 0
    NB = L // blk
    gs = 8
    G = N_GROUPS
    slab = lambda c: pl.BlockSpec((1, blk, LANES), lambda b, n, g: (b, n, c))
    z2, hout = pl.pallas_call(
        functools.partial(_s5_mix_prompt_kernel, nk=nk, gs=gs),
        grid=(B, NB, G // gs),
        in_specs=[slab(c) for c in range(LANE_SLABS)] + [
                  pl.BlockSpec((1, D), lambda b, n, g: (0, 0)),
                  pl.BlockSpec((1, D), lambda b, n, g: (0, 0)),
                  pl.BlockSpec((gs, STATE2, CHUNK_W), lambda b, n, g: (g, 0, 0)),
                  pl.BlockSpec((gs, CHUNK_W, CHUNK_W + STATE2), lambda b, n, g: (g, 0, 0)),
                  pl.BlockSpec((gs, STATE2, ACOLS), lambda b, n, g: (g, 0, 0))],
        out_specs=[pl.BlockSpec((1, nk, CHUNK * D), lambda b, n, g: (b, n, 0)),
                   pl.BlockSpec((1, 1, gs, STATE2, 1), lambda b, n, g: (b, n, g, 0, 0))],
        out_shape=[jax.ShapeDtypeStruct((B, L // CHUNK, CHUNK * D), BF16),
                   jax.ShapeDtypeStruct((B, NB, G, STATE2, 1), F32)],
        scratch_shapes=[pltpu.VMEM((CHUNK, D, nk), BF16),
                        pltpu.VMEM((CHUNK, D, nk), BF16),
                        pltpu.VMEM((CHUNK, nk, 1), F32),
                        pltpu.VMEM((G, STATE2, 1), F32)],
        compiler_params=_cparams(("arbitrary", "arbitrary", "arbitrary")),
        name="s5_mix_prompt",
    )(*([x] * LANE_SLABS), pre_g.reshape(1, D), d_skip.reshape(1, D), wt, tm, acol)
    return z2, hout[:, NB - 1]


def _s5_mix_sample_kernel(x_ref, g_ref, d_ref, h0_ref, wt_ref, tm_ref, acol_ref, z_ref, hout_ref,
                          ut_ref, yt_ref, r_ref, *, nseq, ntok, gs):
    P, D = SSM_STATE, D_MODEL
    half = ntok * SSM_GROUP
    gg = pl.program_id(0)

    @pl.when(gg == 0)
    def _():
        for s in range(ntok):
            xs = x_ref[:, s * D:(s + 1) * D]
            r = _inv_rms(xs)
            r_ref[s] = r
            ut_ref[s] = (xs * r * g_ref[...]).T.astype(BF16)

    for gi in range(gs):
        g = gg * gs + gi
        row0 = pl.multiple_of(g * SSM_GROUP, SSM_GROUP)
        ug = ut_ref[:, pl.ds(row0, SSM_GROUP), :].reshape(half, nseq)
        h0 = h0_ref[gi]
        acol = acol_ref[gi]
        s_all = _dot(wt_ref[gi, :, half:], ug)
        er, ei = _cmul(acol[:P, ACOLS - 1:ACOLS], acol[P:, ACOLS - 1:ACOLS], h0[:P], h0[P:])
        hout_ref[gi] = jnp.concatenate([er + s_all[:P], ei + s_all[P:]], axis=0)
        y = _dot(tm_ref[gi, 0:half, 0:half], ug) + _dot(tm_ref[gi, 0:half, CHUNK_W:], h0.astype(BF16))
        yt_ref[:, pl.ds(row0, SSM_GROUP), :] = y.reshape(ntok, SSM_GROUP, nseq)

    @pl.when(gg == pl.num_programs(0) - 1)
    def _():
        for t in range(ntok):
            xs = x_ref[:, t * D:(t + 1) * D]
            u = xs * r_ref[t] * g_ref[...]
            v = yt_ref[t].T + d_ref[...] * u
            z_ref[:, t * D:(t + 1) * D] = _gelu(v).astype(BF16)


def _s5_mix_sample(x, h0, pre_g, d_skip, wt, tm, acol):
    S, T, D = x.shape
    assert T * 2 == CHUNK
    gs = 8
    G = N_GROUPS
    z2, hout = pl.pallas_call(
        functools.partial(_s5_mix_sample_kernel, nseq=S, ntok=T, gs=gs),
        grid=(G // gs,),
        in_specs=[pl.BlockSpec((S, T * D), lambda g: (0, 0)),
                  pl.BlockSpec((1, D), lambda g: (0, 0)),
                  pl.BlockSpec((1, D), lambda g: (0, 0)),
                  pl.BlockSpec((gs, STATE2, S), lambda g: (g, 0, 0)),
                  pl.BlockSpec((gs, STATE2, CHUNK_W), lambda g: (g, 0, 0)),
                  pl.BlockSpec((gs, CHUNK_W, CHUNK_W + STATE2), lambda g: (g, 0, 0)),
                  pl.BlockSpec((gs, STATE2, ACOLS), lambda g: (g, 0, 0))],
        out_specs=[pl.BlockSpec((S, T * D), lambda g: (0, 0)),
                   pl.BlockSpec((gs, STATE2, S), lambda g: (g, 0, 0))],
        out_shape=[jax.ShapeDtypeStruct((S, T * D), BF16),
                   jax.ShapeDtypeStruct((G, STATE2, S), F32)],
        scratch_shapes=[pltpu.VMEM((T, D, S), BF16),
                        pltpu.VMEM((T, D, S), F32),
                        pltpu.VMEM((T, S, 1), F32)],
        compiler_params=_cparams(("arbitrary",)),
        name="s5_mix_sample",
    )(x.reshape(S, T * D), pre_g.reshape(1, D), d_skip.reshape(1, D), h0, wt, tm, acol)
    return z2, hout


def _tail_kernel(a_ref, w_ref, g_ref, x_ref, o_ref, *, glu):
    y = _dot(a_ref[...], w_ref[...])
    if glu:
        n = y.shape[-1] // 2
        y = y[:, :n] * jax.nn.sigmoid(y[:, n:])
    o_ref[...] = x_ref[...] + y * _inv_rms(y) * g_ref[...]


def _tail(a2, w, g, x2, *, glu, tm):
    R = a2.shape[0]
    K = w.shape[0]
    D = D_MODEL
    n = a2.shape[1] // K
    tm = min(tm, R)
    assert R % tm == 0 and x2.shape == (R, n * D)
    return pl.pallas_call(
        functools.partial(_tail_kernel, glu=glu),
        grid=(R // tm, n),
        in_specs=[pl.BlockSpec((tm, K), lambda i, s: (i, s)),
                  _resident(w.shape),
                  pl.BlockSpec((1, D), lambda i, s: (0, 0)),
                  pl.BlockSpec((tm, D), lambda i, s: (i, s))],
        out_specs=pl.BlockSpec((tm, D), lambda i, s: (i, s)),
        out_shape=jax.ShapeDtypeStruct((R, n * D), F32),
        compiler_params=_cparams(("parallel", "parallel")),
        name="glu_tail" if glu else "oproj_tail",
    )(a2, w, g.reshape(1, D), x2)


def _glu_tail_prompt_kernel(z_ref, w_ref, g_ref, x_ref, o_ref, slab_ref, *, rk, ns):
    D = D_MODEL
    for s0 in range(0, CHUNK, ns):
        a = jnp.concatenate([z_ref[:, s * D:(s + 1) * D] for s in range(s0, s0 + ns)], axis=0)
        y = _dot(a, w_ref[...])
        y = y[:, :D] * jax.nn.sigmoid(y[:, D:])
        y = y * _inv_rms(y) * g_ref[...]
        for j in range(ns):
            for c in range(LANE_SLABS):
                slab_ref[c, pl.ds(s0 + j, rk, stride=CHUNK), :] = y[j * rk:(j + 1) * rk, c * LANES:(c + 1) * LANES]
    for c in range(LANE_SLABS):
        o_ref[:, c * LANES:(c + 1) * LANES] = x_ref[:, c * LANES:(c + 1) * LANES] + slab_ref[c]


def _glu_tail_prompt(z2, w, g, x2, *, rk, ns):
    R = z2.shape[0]
    D = D_MODEL
    assert R % rk == 0 and CHUNK % ns == 0 and x2.shape == (R * CHUNK, D)
    return pl.pallas_call(
        functools.partial(_glu_tail_prompt_kernel, rk=rk, ns=ns),
        grid=(R // rk,),
        in_specs=[pl.BlockSpec((rk, CHUNK * D), lambda i: (i, 0)),
                  _resident(w.shape),
                  pl.BlockSpec((1, D), lambda i: (0, 0)),
                  pl.BlockSpec((rk * CHUNK, D), lambda i: (i, 0))],
        out_specs=pl.BlockSpec((rk * CHUNK, D), lambda i: (i, 0)),
        out_shape=jax.ShapeDtypeStruct((R * CHUNK, D), F32),
        scratch_shapes=[pltpu.VMEM((LANE_SLABS, rk * CHUNK, LANES), F32)],
        compiler_params=_cparams(("parallel",)),
        name="glu_tail_prompt",
    )(z2, w, g.reshape(1, D), x2)


def _kvq_kernel(x_ref, gkv_ref, gq_ref, wk_ref, wv_ref, wq_ref, k_ref, v_ref, *q_refs, by_head, q_scale):
    x = x_ref[...]
    xr = x * _inv_rms(x)
    kv_in = (xr * gkv_ref[...]).astype(BF16)
    xn = (xr * gq_ref[...]).astype(BF16)
    k = _dot(kv_in, wk_ref[...])
    v = _dot(kv_in, wv_ref[...])
    q = _dot(xn, wq_ref[...]) * q_scale
    if by_head:
        for ref, val in ((k_ref, k), (v_ref, v), (q_refs[0], q)):
            for h in range(N_HEADS):
                ref[:, h, :] = val[:, h * V_DIM:(h + 1) * V_DIM]
    else:
        kb_ref, vb_ref, qb_ref = q_refs
        k_ref[...] = k
        v_ref[...] = v
        kb_ref[...] = k.astype(BF16)
        vb_ref[...] = v.astype(BF16)
        qb_ref[...] = q.astype(BF16)


def _kvq(x2, g_kv, g_q, wk, wv, wq, *, tm, by_head, q_scale):
    R, D = x2.shape
    tm = min(tm, R)
    assert R % tm == 0
    tile = pl.BlockSpec((tm, D), lambda i: (i, 0))
    heads = pl.BlockSpec((tm, N_HEADS, V_DIM), lambda i: (i, 0, 0))
    vec = pl.BlockSpec((1, D), lambda i: (0, 0))
    head32 = jax.ShapeDtypeStruct((R, N_HEADS, V_DIM), F32)
    flat32 = jax.ShapeDtypeStruct((R, D), F32)
    flat16 = jax.ShapeDtypeStruct((R, D), BF16)
    return pl.pallas_call(
        functools.partial(_kvq_kernel, by_head=by_head, q_scale=q_scale),
        grid=(R // tm,),
        in_specs=[tile, vec, vec, _resident(wk.shape), _resident(wv.shape), _resident(wq.shape)],
        out_specs=[heads] * 3 if by_head else [tile] * 5,
        out_shape=[head32] * 3 if by_head else [flat32, flat32, flat16, flat16, flat16],
        compiler_params=_cparams(("parallel",)),
        name="kvq_proj",
    )(x2, g_kv.reshape(1, D), g_q.reshape(1, D), wk, wv, wq)


FF_CHUNK = D_FF // 2


def _ffn_cols(c):
    return (slice(c * FF_CHUNK, (c + 1) * FF_CHUNK),
            slice(D_FF + c * FF_CHUNK, D_FF + (c + 1) * FF_CHUNK))


def _ffn_prompt_kernel(x_ref, gpre_ref, gpost_ref, wup_ref, cw_ref, cb_ref, wdn_ref,
                       o_ref, cs_ref, carry_ref, *, tm):
    @pl.when(pl.program_id(1) == 0)
    def _():
        carry_ref[...] = jnp.zeros_like(carry_ref)

    x = x_ref[0]
    xn = (x * _inv_rms(x) * gpre_ref[...]).astype(BF16)
    rows = lax.broadcasted_iota(jnp.int32, (tm, 1), 0)
    f = jnp.zeros((tm, D_MODEL), F32)
    for c in range(D_FF // FF_CHUNK):
        halves = []
        for cols in _ffn_cols(c):
            h = _dot(xn, wup_ref[:, cols])
            c0, c1 = carry_ref[6:7, cols], carry_ref[7:8, cols]
            h1 = jnp.where(rows == 0, c1, pltpu.roll(h, 1, axis=0))
            h2 = jnp.where(rows == 0, c0, jnp.where(rows == 1, c1, pltpu.roll(h, 2, axis=0)))
            halves.append(cb_ref[:, cols] + cw_ref[2:3, cols] * h
                          + cw_ref[1:2, cols] * h1 + cw_ref[0:1, cols] * h2)
            carry_ref[:, cols] = h[tm - 8:tm]
            cs_ref[0, :, cols] = h[tm - 2:tm]
        a = (_gelu(halves[0]) * halves[1]).astype(BF16)
        f = f + _dot(a, wdn_ref[c * FF_CHUNK:(c + 1) * FF_CHUNK, :])
    o_ref[0] = x + f * _inv_rms(f) * gpost_ref[...]


def _ffn_prompt(x, gpre, gpost, wup, cw, cb, wdn, *, tm):
    B, L, D = x.shape
    tm = min(tm, L)
    assert L % tm == 0
    F2 = 2 * D_FF
    vec = pl.BlockSpec((1, D), lambda b, t: (0, 0))
    return pl.pallas_call(
        functools.partial(_ffn_prompt_kernel, tm=tm),
        grid=(B, L // tm),
        in_specs=[pl.BlockSpec((1, tm, D), lambda b, t: (b, t, 0)), vec, vec,
                  _resident(wup.shape), _resident((CONV_W, F2)), _resident((1, F2)),
                  _resident(wdn.shape)],
        out_specs=[pl.BlockSpec((1, tm, D), lambda b, t: (b, t, 0)),
                   pl.BlockSpec((1, CONV_W - 1, F2), lambda b, t: (b, 0, 0))],
        out_shape=[jax.ShapeDtypeStruct((B, L, D), F32),
                   jax.ShapeDtypeStruct((B, CONV_W - 1, F2), F32)],
        scratch_shapes=[pltpu.VMEM((8, F2), F32)],
        compiler_params=_cparams(("arbitrary", "arbitrary")),
        name="ffn_prompt",
    )(x, gpre.reshape(1, D), gpost.reshape(1, D), wup, cw, cb.reshape(1, F2), wdn)


def _ffn_sample_kernel(x_ref, buf_ref, gpre_ref, gpost_ref, wup_ref, cw_ref, cb_ref, wdn_ref,
                       o_ref, cs_ref, *, tm, ntok):
    nsq = tm // ntok
    x = x_ref[...]
    xn = (x * _inv_rms(x) * gpre_ref[...]).astype(BF16)
    sub = lax.broadcasted_iota(jnp.int32, (tm, 1), 0) % ntok
    f = jnp.zeros((tm, D_MODEL), F32)
    for c in range(D_FF // FF_CHUNK):
        halves = []
        for cols in _ffn_cols(c):
            h = _dot(xn, wup_ref[:, cols])
            spread = lambda b: jnp.broadcast_to(b, (nsq, ntok, FF_CHUNK)).reshape(tm, FF_CHUNK)
            b0, b1 = spread(buf_ref[:, 0:1, cols]), spread(buf_ref[:, 1:2, cols])
            h1 = jnp.where(sub == 0, b1, pltpu.roll(h, 1, axis=0))
            h2 = jnp.where(sub == 0, b0, jnp.where(sub == 1, b1, pltpu.roll(h, 2, axis=0)))
            halves.append(cb_ref[:, cols] + cw_ref[2:3, cols] * h
                          + cw_ref[1:2, cols] * h1 + cw_ref[0:1, cols] * h2)
            cs_ref[:, :, cols] = h.reshape(nsq, ntok, FF_CHUNK)[:, ntok - 2:ntok, :]
        a = (_gelu(halves[0]) * halves[1]).astype(BF16)
        f = f + _dot(a, wdn_ref[c * FF_CHUNK:(c + 1) * FF_CHUNK, :])
    o_ref[...] = x + f * _inv_rms(f) * gpost_ref[...]


def _ffn_sample(x, buf, gpre, gpost, wup, cw, cb, wdn, *, tm):
    S, T, D = x.shape
    assert T == 8 and tm % T == 0 and (S * T) % tm == 0
    F2 = 2 * D_FF
    nsq = tm // T
    vec = pl.BlockSpec((1, D), lambda i: (0, 0))
    out, cs = pl.pallas_call(
        functools.partial(_ffn_sample_kernel, tm=tm, ntok=T),
        grid=(S * T // tm,),
        in_specs=[pl.BlockSpec((tm, D), lambda i: (i, 0)),
                  pl.BlockSpec((nsq, CONV_W - 1, F2), lambda i: (i, 0, 0)), vec, vec,
                  _resident(wup.shape), _resident((CONV_W, F2)), _resident((1, F2)),
                  _resident(wdn.shape)],
        out_specs=[pl.BlockSpec((tm, D), lambda i: (i, 0)),
                   pl.BlockSpec((nsq, CONV_W - 1, F2), lambda i: (i, 0, 0))],
        out_shape=[jax.ShapeDtypeStruct((S * T, D), F32),
                   jax.ShapeDtypeStruct((S, CONV_W - 1, F2), F32)],
        compiler_params=_cparams(("parallel",)),
        name="ffn_sample",
    )(x.reshape(S * T, D), buf, gpre.reshape(1, D), gpost.reshape(1, D), wup, cw,
      cb.reshape(1, F2), wdn)
    return out.reshape(S, T, D), cs


def _lam_value(lq1, lk1, lq2, lk2, lam_init):
    return (jnp.exp(jnp.sum(lq1[...] * lk1[...], axis=-1, keepdims=True))
            - jnp.exp(jnp.sum(lq2[...] * lk2[...], axis=-1, keepdims=True)) + lam_init)


def _sub_norm(o, subg, lam_init):
    return o * _inv_rms(o) * subg * (1.0 - lam_init)


LOG2E = math.log2(math.e)
FLASH_ROWS = 256
FLASH_AHEAD = 1
POS_SPLIT = 64


def _flash_kernel(qi_ref, ki_ref, q_ref, k_ref, v_ref, lq1, lk1, lq2, lk2, subg_ref, o_ref,
                  q2_ref, kf_ref, m_ref, acc_ref, *, tq, lam_init):
    h, step = pl.program_id(1), pl.program_id(2)
    qi, ki = qi_ref[step], ki_ref[step]
    RQ = FLASH_ROWS
    n_chunks = tq // RQ
    lane = lax.broadcasted_iota(jnp.int32, (1, V_DIM), 1)
    slope2 = jnp.exp2(-jnp.full((1, V_DIM), h + 1, jnp.int32).astype(F32)) * LOG2E

    @pl.when(ki == 0)
    def _():
        m_ref[...] = jnp.full_like(m_ref, -jnp.inf)
        acc_ref[...] = jnp.zeros_like(acc_ref)
        kidx = lax.broadcasted_iota(jnp.int32, (tq, 1), 0)
        a, b = (kidx // POS_SPLIT).astype(F32), (kidx % POS_SPLIT).astype(F32)
        kf_ref[...] = jnp.where(lane < 3, a, jnp.where(lane < 6, b, 0.0)).astype(BF16)
        c0 = slope2.astype(BF16).astype(F32)
        c1 = (slope2 - c0).astype(BF16).astype(F32)
        c2 = (slope2 - c0 - c1).astype(BF16).astype(F32)
        part = jnp.where(lane % 3 == 0, c0, jnp.where(lane % 3 == 1, c1, c2))
        qf = jnp.where(lane < 3, POS_SPLIT * part, jnp.where(lane < 6, part, 0.0)).astype(BF16)
        qf = jnp.broadcast_to(qf, (RQ, V_DIM))
        for c_i in range(n_chunks):
            q = q_ref[c_i * RQ:(c_i + 1) * RQ, :]
            zero = jnp.zeros_like(q)
            q2_ref[c_i, 0:RQ, 0:V_DIM] = jnp.where(lane < HEAD_DIM, q, zero)
            q2_ref[c_i, RQ:2 * RQ, 0:V_DIM] = jnp.where(lane >= HEAD_DIM, q, zero)
            q2_ref[c_i, 0:RQ, V_DIM:] = qf
            q2_ref[c_i, RQ:2 * RQ, V_DIM:] = qf

    def absorb(diagonal):
        k_aug = jnp.concatenate([k_ref[...], kf_ref[...]], axis=1)
        v_aug = jnp.concatenate([v_ref[...], jnp.ones((tq, V_DIM), BF16)], axis=1)
        shift = slope2 * ((ki - qi) * tq).astype(F32)
        n_keys = lambda c: (c + 1) * RQ if diagonal else tq
        scores = lambda c: lax.dot_general(q2_ref[c], k_aug[0:n_keys(c)], (((1,), (1,)), ((), ())),
                                           preferred_element_type=F32)
        ahead = [scores(c) for c in range(min(FLASH_AHEAD, n_chunks))]
        for c_i in range(n_chunks):
            nkeys = n_keys(c_i)
            s = ahead.pop(0)
            if c_i + FLASH_AHEAD < n_chunks:
                ahead.append(scores(c_i + FLASH_AHEAD))
            if diagonal:
                kcol = lax.broadcasted_iota(jnp.int32, (1, RQ), 1)
                qrow = lax.broadcasted_iota(jnp.int32, (2 * RQ, 1), 0) % RQ
                tail = jnp.where(kcol <= qrow, s[:, nkeys - RQ:], -jnp.inf)
                s = tail if nkeys == RQ else jnp.concatenate([s[:, 0:nkeys - RQ], tail], axis=1)
            m_old = m_ref[c_i]
            m_new = jnp.maximum(m_old, jnp.max(s, axis=-1, keepdims=True) + shift)
            alpha = jnp.exp2(m_old - m_new)
            p = jnp.exp2(s - jnp.concatenate([m_new - shift] * (nkeys // V_DIM), axis=1))
            pv = _dot(p.astype(BF16), v_aug[0:nkeys])
            acc_ref[c_i] = jnp.concatenate([alpha, alpha], axis=1) * acc_ref[c_i] + pv
            m_ref[c_i] = m_new

    @pl.when(ki < qi)
    def _():
        absorb(False)

    @pl.when(ki == qi)
    def _():
        absorb(True)
        lam = _lam_value(lq1, lk1, lq2, lk2, lam_init)
        for c_i in range(n_chunks):
            acc = acc_ref[c_i]
            o = (acc[0:RQ, 0:V_DIM] / acc[0:RQ, V_DIM:]
                 - lam * (acc[RQ:, 0:V_DIM] / acc[RQ:, V_DIM:]))
            o_ref[c_i * RQ:(c_i + 1) * RQ, :] = _sub_norm(o, subg_ref[...], lam_init).astype(o_ref.dtype)


def _flash_prompt(qb, kb, vb, lq1, lk1, lq2, lk2, subg, *, B, L, tq, lam_init):
    tq = min(tq, L)
    assert L % tq == 0 and tq % FLASH_ROWS == 0 and tq <= POS_SPLIT * POS_SPLIT
    nq = L // tq
    n_chunks = tq // FLASH_ROWS
    pairs = [(i, j) for i in range(nq) for j in range(i + 1)]
    qi_tab = jnp.asarray([p[0] for p in pairs], jnp.int32)
    ki_tab = jnp.asarray([p[1] for p in pairs], jnp.int32)
    lvec = pl.BlockSpec((1, HEAD_DIM), lambda b, h, s, qt, kt: (0, 0))
    grid_spec = pltpu.PrefetchScalarGridSpec(
        num_scalar_prefetch=2,
        grid=(B, N_HEADS, len(pairs)),
        in_specs=[pl.BlockSpec((tq, V_DIM), lambda b, h, s, qt, kt: (b * nq + qt[s], h)),
                  pl.BlockSpec((tq, V_DIM), lambda b, h, s, qt, kt: (b * nq + kt[s], h)),
                  pl.BlockSpec((tq, V_DIM), lambda b, h, s, qt, kt: (b * nq + kt[s], h)),
                  lvec, lvec, lvec, lvec,
                  pl.BlockSpec((1, V_DIM), lambda b, h, s, qt, kt: (0, 0))],
        out_specs=pl.BlockSpec((tq, V_DIM), lambda b, h, s, qt, kt: (b * nq + qt[s], h)),
        scratch_shapes=[pltpu.VMEM((n_chunks, 2 * FLASH_ROWS, 2 * V_DIM), BF16),
                        pltpu.VMEM((tq, V_DIM), BF16),
                        pltpu.VMEM((n_chunks, 2 * FLASH_ROWS, V_DIM), F32),
                        pltpu.VMEM((n_chunks, 2 * FLASH_ROWS, 2 * V_DIM), F32)])
    return pl.pallas_call(
        functools.partial(_flash_kernel, tq=tq, lam_init=lam_init),
        grid_spec=grid_spec,
        out_shape=jax.ShapeDtypeStruct((B * L, N_HEADS * V_DIM), BF16),
        compiler_params=_cparams(("parallel", "parallel", "arbitrary")),
        name="flash_prompt",
    )(qi_tab, ki_tab, qb, kb, vb, lq1, lk1, lq2, lk2, subg)


def _paged_kernel(pt_ref, q_ref, *refs, ntok, page, npp, past_len, lam_init):
    kc = refs[:npp]
    vc = refs[npp:2 * npp]
    (kn_ref, vn_ref, lq1, lk1, lq2, lk2, subg_ref, o_ref,
     qall_ref, bias_ref, m_ref, l_ref, acc_ref) = refs[2 * npp:]
    H = N_HEADS
    R = 2 * ntok * H
    PW = page * H
    pg = pl.program_id(1)

    row = lax.broadcasted_iota(jnp.int32, (R, 1), 0)
    slope_r = jnp.exp2(-(row % H + 1).astype(F32)) * LOG2E
    qi_r = (row // H) % ntok

    def head_bias(n_lanes, key_limit):
        lane = lax.broadcasted_iota(jnp.int32, (1, n_lanes), 1)
        key = lane // H
        ok = (lane % H == row % H) & (key < key_limit)
        return jnp.where(ok, slope_r * (key - qi_r).astype(F32), -jnp.inf)

    @pl.when(pg == 0)
    def _():
        m_ref[...] = jnp.full_like(m_ref, -jnp.inf)
        l_ref[...] = jnp.zeros_like(l_ref)
        acc_ref[...] = jnp.zeros_like(acc_ref)
        q2 = q_ref[...].reshape(ntok * H, V_DIM)
        col = lax.broadcasted_iota(jnp.int32, (1, V_DIM), 1)
        qall_ref[...] = jnp.concatenate([jnp.where(col < HEAD_DIM, q2, 0.0),
                                         jnp.where(col >= HEAD_DIM, q2, 0.0)], axis=0).astype(BF16)
        bias_ref[...] = head_bias(PW, page)

    def absorb(ks, vs, bias, shifts):
        qall = qall_ref[...]
        s = [lax.dot_general(qall, k, (((1,), (1,)), ((), ())), preferred_element_type=F32) + bias
             for k in ks]
        m_old = m_ref[...]
        m_new = m_old
        for si, sh in zip(s, shifts):
            m_new = jnp.maximum(m_new, jnp.max(si, axis=-1, keepdims=True) + sh)
        alpha = jnp.exp2(m_old - m_new)
        l_new = alpha * l_ref[...]
        acc = alpha * acc_ref[...]
        for si, sh, v in zip(s, shifts, vs):
            p = jnp.exp2(si - (m_new - sh))
            l_new = l_new + jnp.sum(p, axis=-1, keepdims=True)
            acc = acc + _dot(p.astype(BF16), v)
        m_ref[...] = m_new
        l_ref[...] = l_new
        acc_ref[...] = acc

    flat = lambda ref: ref[0].reshape(PW, V_DIM).astype(BF16)
    shifts = [slope_r * ((pg * npp + i) * page - past_len).astype(F32) for i in range(npp)]
    absorb([flat(r) for r in kc], [flat(r) for r in vc], bias_ref[...], shifts)

    @pl.when(pg == pl.num_programs(1) - 1)
    def _():
        n_new = ntok * H
        pad = jnp.zeros((R - n_new, V_DIM), F32)
        kn = jnp.concatenate([kn_ref[...].reshape(n_new, V_DIM), pad], axis=0).astype(BF16)
        vn = jnp.concatenate([vn_ref[...].reshape(n_new, V_DIM), pad], axis=0).astype(BF16)
        absorb([kn], [vn], head_bias(R, jnp.minimum(qi_r + 1, ntok)), [0.0])
        lam = _lam_value(lq1, lk1, lq2, lk2, lam_init)
        h = R // 2
        o = acc_ref[0:h] / l_ref[0:h] - lam * (acc_ref[h:R] / l_ref[h:R])
        o_ref[...] = _sub_norm(o, subg_ref[...], lam_init).reshape(ntok, H, V_DIM)


def _paged_sample(page_table, q, cache_k, cache_v, k_new, v_new, lq1, lk1, lq2, lk2, subg,
                  *, ntok, npp, lam_init):
    S, n_pages = page_table.shape
    page = cache_k.shape[1]
    H = N_HEADS
    R = 2 * ntok * H
    assert n_pages % npp == 0
    tok = pl.BlockSpec((ntok, H, V_DIM), lambda s, p, pt: (s, 0, 0))
    pgs = [pl.BlockSpec((1, page, H, V_DIM), functools.partial(
        lambda s, p, pt, i: (pt[s, p * npp + i], 0, 0, 0), i=i)) for i in range(npp)]
    lvec = pl.BlockSpec((1, HEAD_DIM), lambda s, p, pt: (0, 0))
    grid_spec = pltpu.PrefetchScalarGridSpec(
        num_scalar_prefetch=1,
        grid=(S, n_pages // npp),
        in_specs=[tok] + pgs + pgs + [tok, tok, lvec, lvec, lvec, lvec,
                                      pl.BlockSpec((1, V_DIM), lambda s, p, pt: (0, 0))],
        out_specs=tok,
        scratch_shapes=[pltpu.VMEM((R, V_DIM), BF16), pltpu.VMEM((R, page * H), F32),
                        pltpu.VMEM((R, 1), F32), pltpu.VMEM((R, 1), F32), pltpu.VMEM((R, V_DIM), F32)])
    return pl.pallas_call(
        functools.partial(_paged_kernel, ntok=ntok, page=page, npp=npp, past_len=n_pages * page,
                          lam_init=lam_init),
        grid_spec=grid_spec,
        out_shape=jax.ShapeDtypeStruct((S * ntok, H, V_DIM), F32),
        compiler_params=_cparams(("arbitrary", "arbitrary")),
        name="paged_sample",
    )(page_table, q, *([cache_k] * npp), *([cache_v] * npp), k_new, v_new, lq1, lk1, lq2, lk2, subg)


def kernel(x_prompt, x_sample, state_ssm_re, state_ssm_im, state_conv, cache_k, cache_v, page_table,
           a_pre_g, a_post_g, ssm_lam_re, ssm_lam_im, ssm_log_dt, ssm_b_re, ssm_b_im, ssm_c_re,
           ssm_c_im, ssm_d, glu_w, kv_norm_g, w_k, w_v, b_pre_g, b_post_g, w_q, lam_q1, lam_k1,
           lam_q2, lam_k2, sub_g, w_o, f_pre_g, f_post_g, w_up, conv_w, conv_b, w_down):
    B, L, D = x_prompt.shape
    S, T, _ = x_sample.shape
    P, G = SSM_STATE, N_GROUPS
    HW = N_HEADS * V_DIM
    st_dtype = state_ssm_re.dtype

    wt, tm_op, acol = _s5_prepare(ssm_lam_re[0], ssm_lam_im[0], ssm_log_dt[0], ssm_b_re[0],
                                  ssm_b_im[0], ssm_c_re[0], ssm_c_im[0])
    glu_b = glu_w[0].astype(BF16)
    zp, hp = _s5_mix_prompt(x_prompt, a_pre_g[0], ssm_d[0], wt, tm_op, acol)
    xp = _glu_tail_prompt(zp.reshape(B * L // CHUNK, CHUNK * D), glu_b, a_post_g[0],
                          x_prompt.reshape(B * L, D), rk=64, ns=8).reshape(B, L, D)
    h0 = jnp.concatenate([state_ssm_re[0].astype(F32), state_ssm_im[0].astype(F32)], axis=-1)
    zs, hs = _s5_mix_sample(x_sample, h0.transpose(1, 2, 0), a_pre_g[0], ssm_d[0], wt, tm_op, acol)
    xs = _tail(zs, glu_b, a_post_g[0], x_sample.reshape(S, T * D), glu=True, tm=512).reshape(S, T, D)

    hp = hp.reshape(B, G, 2 * P)
    hs = hs.transpose(2, 0, 1)
    ssm_re_p, ssm_im_p = hp[None, ..., :P].astype(st_dtype), hp[None, ..., P:].astype(st_dtype)
    ssm_re_s, ssm_im_s = hs[None, ..., :P].astype(st_dtype), hs[None, ..., P:].astype(st_dtype)

    wup_b, wdn_b = w_up.astype(BF16), w_down.astype(BF16)
    xp, conv_p0 = _ffn_prompt(xp, f_pre_g[0], f_post_g[0], wup_b[0], conv_w[0], conv_b[0], wdn_b[0], tm=512)
    xs, conv_s0 = _ffn_sample(xs, state_conv[0], f_pre_g[0], f_post_g[0], wup_b[0], conv_w[0],
                              conv_b[0], wdn_b[0], tm=512)

    lam_init = 0.8 - 0.6 * math.exp(-0.3 * N_A_LAYERS)
    wk_b, wv_b, wq_b, wo_b = w_k.astype(BF16), w_v.astype(BF16), w_q[0].astype(BF16), w_o[0].astype(BF16)
    lvec = lambda a: a[0].reshape(1, HEAD_DIM).astype(F32)
    lams = (lvec(lam_q1), lvec(lam_k1), lvec(lam_q2), lvec(lam_k2))
    subg = sub_g[0].reshape(1, V_DIM)

    xp2 = xp.reshape(B * L, D)
    kp, vp, kpb, vpb, qpb = _kvq(xp2, kv_norm_g, b_pre_g[0], wk_b, wv_b, wq_b, tm=512, by_head=False,
                                 q_scale=LOG2E * HEAD_DIM ** -0.5)
    op = _flash_prompt(qpb, kpb, vpb, *lams, subg, B=B, L=L, tq=2048, lam_init=lam_init)
    xp = _tail(op, wo_b, b_post_g[0], xp2, glu=False, tm=512).reshape(B, L, D)

    xs2 = xs.reshape(S * T, D)
    ks, vs, qs = _kvq(xs2, kv_norm_g, b_pre_g[0], wk_b, wv_b, wq_b, tm=512, by_head=True,
                      q_scale=LOG2E * HEAD_DIM ** -0.5)
    os_ = _paged_sample(page_table, qs, cache_k, cache_v, ks, vs, *lams, subg, ntok=T, npp=16,
                        lam_init=lam_init)
    xs = _tail(os_.reshape(S * T, HW).astype(BF16), wo_b, b_post_g[0], xs2, glu=False, tm=512).reshape(S, T, D)

    xp, conv_p1 = _ffn_prompt(xp, f_pre_g[1], f_post_g[1], wup_b[1], conv_w[1], conv_b[1], wdn_b[1], tm=512)
    xs, conv_s1 = _ffn_sample(xs, state_conv[1], f_pre_g[1], f_post_g[1], wup_b[1], conv_w[1],
                              conv_b[1], wdn_b[1], tm=512)

    return (xp, xs, ssm_re_p, ssm_im_p, jnp.stack([conv_p0, conv_p1]),
            kp.reshape(B, L, N_HEADS, 2 * HEAD_DIM), vp.reshape(B, L, N_HEADS, V_DIM),
            ssm_re_s, ssm_im_s, jnp.stack([conv_s0, conv_s1]),
            ks.reshape(S, T, N_HEADS, 2 * HEAD_DIM), vs.reshape(S, T, N_HEADS, V_DIM))
```

```python
import functools
import math

import jax
import jax.numpy as jnp
from jax import lax
from jax.experimental import pallas as pl
from jax.experimental.pallas import tpu as pltpu

D_MODEL = 1024
SSM_GROUP = 16
N_GROUPS = D_MODEL // SSM_GROUP
SSM_STATE = 64
N_HEADS = 8
HEAD_DIM = 64
V_DIM = 2 * HEAD_DIM
D_FF = 2816
CONV_W = 3
NORM_EPS = 1e-6
N_A_LAYERS = 1

LANES = 128
LANE_SLABS = D_MODEL // LANES
CHUNK = 16
STATE2 = 2 * SSM_STATE
CHUNK_W = CHUNK * SSM_GROUP
SCAN_STEPS = 7
ACOLS = 8
PREP_GROUPS = 16
HIGHEST = lax.Precision.HIGHEST
BF16 = jnp.bfloat16
F32 = jnp.float32
VMEM_LIMIT = 56 * 1024 * 1024


def _cparams(sem, vmem=VMEM_LIMIT):
    return pltpu.CompilerParams(dimension_semantics=sem, vmem_limit_bytes=vmem)


def _resident(shape):
    zeros = (0,) * len(shape)
    return pl.BlockSpec(shape, lambda *_: zeros, pipeline_mode=pl.Buffered(1))


def _inv_rms(x):
    return lax.rsqrt(jnp.mean(x * x, axis=-1, keepdims=True) + NORM_EPS)


def _gelu(x):
    c = math.sqrt(2.0 / math.pi)
    return x * (0.5 * (1.0 + jnp.tanh(c * (x + 0.044715 * (x * x * x)))))


def _dot(a, b):
    return jnp.dot(a, b, preferred_element_type=F32)


def _s5_prep_kernel(lre, lim, ldt, bre, bim, cre, cim, wt_ref, tm_ref, acol_ref, w_scr, z_scr):
    P, C, GB = SSM_STATE, SSM_GROUP, PREP_GROUPS
    dt = jnp.exp(ldt[...])
    lr, li = lre[...], lim[...]
    ar, ai = lr * dt, li * dt
    first = lax.broadcasted_iota(jnp.int32, (1, 2 * P), 1) < P
    sgn = jnp.where(first, -1.0, 1.0)

    def cpow(n):
        m = jnp.exp(n * ar)
        pr, pi = m * jnp.cos(n * ai), m * jnp.sin(n * ai)
        return jnp.where(first, pr, pi), jnp.where(first, pi, pr)

    def cmul(x1, x2, yr, yi):
        return x1[:, None, :] * yr + (sgn * x2)[:, None, :] * yi

    lbr, lbi = jnp.exp(ar) * jnp.cos(ai), jnp.exp(ar) * jnp.sin(ai)
    den = lr * lr + li * li
    nr, ni = lbr - 1.0, lbi
    cr = (nr * lr + ni * li) / den
    ci = (ni * lr - nr * li) / den
    bbr = cr[:, None, :] * bre[...] - ci[:, None, :] * bim[...]
    bbi = cr[:, None, :] * bim[...] + ci[:, None, :] * bre[...]
    for s in range(CHUNK):
        w_scr[:, s * C:(s + 1) * C, :] = cmul(*cpow(float(CHUNK - 1 - s)), bbr, bbi)
    for i in range(ACOLS):
        n = float(CHUNK * 2 ** i) if i < SCAN_STEPS else float(CHUNK // 2)
        acol_ref[i] = cpow(n)[0]
    for m in range(CHUNK + 1):
        z_scr[:, m * C:(m + 1) * C, :] = -sgn * cmul(*cpow(float(m)), cre[...], cim[...])
    for g in range(GB):
        wt_ref[g] = w_scr[g].T.astype(BF16)
    bb = jnp.where(first, bbr, bbi)
    bb_rep = jnp.concatenate([bb] * CHUNK, axis=1)
    kw = lax.dot_general(z_scr[:, 0:CHUNK_W, :], bb_rep, (((2,), (2,)), ((0,), (0,))), precision=HIGHEST,
                         preferred_element_type=F32)
    blk = lax.broadcasted_iota(jnp.int32, (1, 1, CHUNK_W), 2) // C
    toe = jnp.where(blk == 0, kw, 0.0)
    for s in range(1, CHUNK):
        down = jnp.concatenate([jnp.zeros((GB, s * C, CHUNK_W), F32), kw[:, 0:CHUNK_W - s * C, :]], axis=1)
        toe = jnp.where(blk == s, down, toe)
    tm_ref[:, :, 0:CHUNK_W] = toe.astype(BF16)
    tm_ref[:, :, CHUNK_W:] = z_scr[:, C:, :].astype(BF16)


def _s5_prepare(lam_re, lam_im, log_dt, b_re, b_im, c_re, c_im):
    G, P, C, GB = N_GROUPS, SSM_STATE, SSM_GROUP, PREP_GROUPS
    blk = lambda *shape: pl.BlockSpec((GB,) + shape, lambda i: (i,) + (0,) * len(shape))
    dup = lambda a: jnp.concatenate([a, a], axis=-1)
    wt, tm, acol = pl.pallas_call(
        _s5_prep_kernel,
        grid=(G // GB,),
        in_specs=[blk(2 * P), blk(2 * P), blk(1), blk(C, 2 * P), blk(C, 2 * P), blk(C, 2 * P), blk(C, 2 * P)],
        out_specs=[blk(STATE2, CHUNK_W), blk(CHUNK_W, CHUNK_W + STATE2),
                   pl.BlockSpec((ACOLS, GB, 2 * P), lambda i: (0, i, 0))],
        out_shape=[jax.ShapeDtypeStruct((G, STATE2, CHUNK_W), BF16),
                   jax.ShapeDtypeStruct((G, CHUNK_W, CHUNK_W + STATE2), BF16),
                   jax.ShapeDtypeStruct((ACOLS, G, 2 * P), F32)],
        scratch_shapes=[pltpu.VMEM((GB, CHUNK_W, 2 * P), F32), pltpu.VMEM((GB, CHUNK_W + C, 2 * P), F32)],
        compiler_params=_cparams(("parallel",)),
        name="s5_prep",
    )(dup(lam_re), dup(lam_im), log_dt.reshape(G, 1), dup(b_re.transpose(0, 2, 1)),
      dup(b_im.transpose(0, 2, 1)), dup(c_re), dup(c_im))
    return wt, tm, acol.transpose(1, 2, 0)


def _cmul(ar, ai, br, bi):
    return ar * br - ai * bi, ar * bi + ai * br


def _s5_mix_prompt_kernel(*refs, nk, gs):
    x_refs = refs[:LANE_SLABS]
    (g_ref, d_ref, wt_ref, tm_ref, acol_ref, z_ref, hout_ref,
     ut_ref, yt_ref, r_ref, carry_ref) = refs[LANE_SLABS:]
    P, D = SSM_STATE, D_MODEL
    nb, gg = pl.program_id(1), pl.program_id(2)

    def slot(s):
        return jnp.concatenate([x[0, pl.ds(s, nk, stride=CHUNK), :] for x in x_refs], axis=1)

    @pl.when(gg == 0)
    def _():
        @pl.when(nb == 0)
        def _():
            carry_ref[...] = jnp.zeros_like(carry_ref)
        for s in range(CHUNK):
            xs = slot(s)
            r = _inv_rms(xs)
            r_ref[s] = r
            ut_ref[s] = (xs * r * g_ref[...]).T.astype(BF16)

    lane = lax.broadcasted_iota(jnp.int32, (gs * P, nk), 1)
    row0 = pl.multiple_of(gg * (gs * SSM_GROUP), gs * SSM_GROUP)
    u_all = ut_ref[:, pl.ds(row0, gs * SSM_GROUP), :]
    ugs = [u_all[:, gi * SSM_GROUP:(gi + 1) * SSM_GROUP, :].reshape(CHUNK_W, nk) for gi in range(gs)]
    s_loc = [_dot(wt_ref[gi], ugs[gi]) for gi in range(gs)]
    stack = lambda parts: jnp.concatenate(parts, axis=0)
    hin = carry_ref[pl.ds(gg * gs, gs)]
    hin_r, hin_i = hin[:, :P].reshape(gs * P, 1), hin[:, P:].reshape(gs * P, 1)
    acol = acol_ref[...]
    a_r, a_i = acol[:, :P].reshape(gs * P, ACOLS), acol[:, P:].reshape(gs * P, ACOLS)
    cr, ci = _cmul(a_r[:, 0:1], a_i[:, 0:1], hin_r, hin_i)
    sr = stack([s[:P] for s in s_loc]) + jnp.where(lane == 0, cr, 0.0)
    si = stack([s[P:] for s in s_loc]) + jnp.where(lane == 0, ci, 0.0)
    for i in range(SCAN_STEPS):
        sh = 1 << i
        pr = jnp.where(lane >= sh, pltpu.roll(sr, sh, axis=1), 0.0)
        pi = jnp.where(lane >= sh, pltpu.roll(si, sh, axis=1), 0.0)
        qr, qi = _cmul(a_r[:, i:i + 1], a_i[:, i:i + 1], pr, pi)
        sr, si = sr + qr, si + qi
    hpr = jnp.where(lane >= 1, pltpu.roll(sr, 1, axis=1), hin_r)
    hpi = jnp.where(lane >= 1, pltpu.roll(si, 1, axis=1), hin_i)
    end_r, end_i = sr[:, nk - 1:nk].reshape(gs, P, 1), si[:, nk - 1:nk].reshape(gs, P, 1)
    hend = jnp.concatenate([end_r, end_i], axis=1)
    carry_ref[pl.ds(gg * gs, gs)] = hend
    hout_ref[0, 0] = hend
    ys = []
    for gi in range(gs):
        rows = slice(gi * P, (gi + 1) * P)
        hprev = stack([hpr[rows], hpi[rows]]).astype(BF16)
        y = _dot(tm_ref[gi, :, 0:CHUNK_W], ugs[gi]) + _dot(tm_ref[gi, :, CHUNK_W:], hprev)
        ys.append(y.reshape(CHUNK, SSM_GROUP, nk).astype(BF16))
    yt_ref[:, pl.ds(row0, gs * SSM_GROUP), :] = jnp.concatenate(ys, axis=1)

    @pl.when(gg == pl.num_programs(2) - 1)
    def _():
        for t in range(CHUNK):
            u = slot(t) * r_ref[t] * g_ref[...]
            v = yt_ref[t].astype(F32).T + d_ref[...] * u
            z_ref[0, :, t * D:(t + 1) * D] = _gelu(v).astype(BF16)


def _s5_mix_prompt(x, pre_g, d_skip, wt, tm, acol):
    B, L, D = x.shape
    nk = 1 << SCAN_STEPS
    blk = nk * CHUNK
    assert L % blk == 0
    NB = L // blk
    gs = 8
    G = N_GROUPS
    slab = lambda c: pl.BlockSpec((1, blk, LANES), lambda b, n, g: (b, n, c))
    z2, hout = pl.pallas_call(
        functools.partial(_s5_mix_prompt_kernel, nk=nk, gs=gs),
        grid=(B, NB, G // gs),
        in_specs=[slab(c) for c in range(LANE_SLABS)] + [
                  pl.BlockSpec((1, D), lambda b, n, g: (0, 0)),
                  pl.BlockSpec((1, D), lambda b, n, g: (0, 0)),
                  pl.BlockSpec((gs, STATE2, CHUNK_W), lambda b, n, g: (g, 0, 0)),
                  pl.BlockSpec((gs, CHUNK_W, CHUNK_W + STATE2), lambda b, n, g: (g, 0, 0)),
                  pl.BlockSpec((gs, STATE2, ACOLS), lambda b, n, g: (g, 0, 0))],
        out_specs=[pl.BlockSpec((1, nk, CHUNK * D), lambda b, n, g: (b, n, 0)),
                   pl.BlockSpec((1, 1, gs, STATE2, 1), lambda b, n, g: (b, n, g, 0, 0))],
        out_shape=[jax.ShapeDtypeStruct((B, L // CHUNK, CHUNK * D), BF16),
                   jax.ShapeDtypeStruct((B, NB, G, STATE2, 1), F32)],
        scratch_shapes=[pltpu.VMEM((CHUNK, D, nk), BF16),
                        pltpu.VMEM((CHUNK, D, nk), BF16),
                        pltpu.VMEM((CHUNK, nk, 1), F32),
                        pltpu.VMEM((G, STATE2, 1), F32)],
        compiler_params=_cparams(("arbitrary", "arbitrary", "arbitrary")),
        name="s5_mix_prompt",
    )(*([x] * LANE_SLABS), pre_g.reshape(1, D), d_skip.reshape(1, D), wt, tm, acol)
    return z2, hout[:, NB - 1]


def _s5_mix_sample_kernel(x_ref, g_ref, d_ref, h0_ref, wt_ref, tm_ref, acol_ref, z_ref, hout_ref,
                          ut_ref, yt_ref, r_ref, *, nseq, ntok, gs):
    P, D = SSM_STATE, D_MODEL
    half = ntok * SSM_GROUP
    gg = pl.program_id(0)

    @pl.when(gg == 0)
    def _():
        for s in range(ntok):
            xs = x_ref[:, s * D:(s + 1) * D]
            r = _inv_rms(xs)
            r_ref[s] = r
            ut_ref[s] = (xs * r * g_ref[...]).T.astype(BF16)

    for gi in range(gs):
        g = gg * gs + gi
        row0 = pl.multiple_of(g * SSM_GROUP, SSM_GROUP)
        ug = ut_ref[:, pl.ds(row0, SSM_GROUP), :].reshape(half, nseq)
        h0 = h0_ref[gi]
        acol = acol_ref[gi]
        s_all = _dot(wt_ref[gi, :, half:], ug)
        er, ei = _cmul(acol[:P, ACOLS - 1:ACOLS], acol[P:, ACOLS - 1:ACOLS], h0[:P], h0[P:])
        hout_ref[gi] = jnp.concatenate([er + s_all[:P], ei + s_all[P:]], axis=0)
        y = _dot(tm_ref[gi, 0:half, 0:half], ug) + _dot(tm_ref[gi, 0:half, CHUNK_W:], h0.astype(BF16))
        yt_ref[:, pl.ds(row0, SSM_GROUP), :] = y.reshape(ntok, SSM_GROUP, nseq)

    @pl.when(gg == pl.num_programs(0) - 1)
    def _():
        for t in range(ntok):
            xs = x_ref[:, t * D:(t + 1) * D]
            u = xs * r_ref[t] * g_ref[...]
            v = yt_ref[t].T + d_ref[...] * u
            z_ref[:, t * D:(t + 1) * D] = _gelu(v).astype(BF16)


def _s5_mix_sample(x, h0, pre_g, d_skip, wt, tm, acol):
    S, T, D = x.shape
    assert T * 2 == CHUNK
    gs = 8
    G = N_GROUPS
    z2, hout = pl.pallas_call(
        functools.partial(_s5_mix_sample_kernel, nseq=S, ntok=T, gs=gs),
        grid=(G // gs,),
        in_specs=[pl.BlockSpec((S, T * D), lambda g: (0, 0)),
                  pl.BlockSpec((1, D), lambda g: (0, 0)),
                  pl.BlockSpec((1, D), lambda g: (0, 0)),
                  pl.BlockSpec((gs, STATE2, S), lambda g: (g, 0, 0)),
                  pl.BlockSpec((gs, STATE2, CHUNK_W), lambda g: (g, 0, 0)),
                  pl.BlockSpec((gs, CHUNK_W, CHUNK_W + STATE2), lambda g: (g, 0, 0)),
                  pl.BlockSpec((gs, STATE2, ACOLS), lambda g: (g, 0, 0))],
        out_specs=[pl.BlockSpec((S, T * D), lambda g: (0, 0)),
                   pl.BlockSpec((gs, STATE2, S), lambda g: (g, 0, 0))],
        out_shape=[jax.ShapeDtypeStruct((S, T * D), BF16),
                   jax.ShapeDtypeStruct((G, STATE2, S), F32)],
        scratch_shapes=[pltpu.VMEM((T, D, S), BF16),
                        pltpu.VMEM((T, D, S), F32),
                        pltpu.VMEM((T, S, 1), F32)],
        compiler_params=_cparams(("arbitrary",)),
        name="s5_mix_sample",
    )(x.reshape(S, T * D), pre_g.reshape(1, D), d_skip.reshape(1, D), h0, wt, tm, acol)
    return z2, hout


def _glu_tail_kernel(a_ref, w_ref, g_ref, x_ref, o_ref):
    y = _dot(a_ref[...], w_ref[...])
    n = y.shape[-1] // 2
    y = y[:, :n] * jax.nn.sigmoid(y[:, n:])
    o_ref[...] = x_ref[...] + y * _inv_rms(y) * g_ref[...]


def _glu_tail(a2, w, g, x2, *, tm):
    R = a2.shape[0]
    K = w.shape[0]
    D = D_MODEL
    n = a2.shape[1] // K
    tm = min(tm, R)
    assert R % tm == 0 and x2.shape == (R, n * D)
    return pl.pallas_call(
        _glu_tail_kernel,
        grid=(R // tm, n),
        in_specs=[pl.BlockSpec((tm, K), lambda i, s: (i, s)),
                  _resident(w.shape),
                  pl.BlockSpec((1, D), lambda i, s: (0, 0)),
                  pl.BlockSpec((tm, D), lambda i, s: (i, s))],
        out_specs=pl.BlockSpec((tm, D), lambda i, s: (i, s)),
        out_shape=jax.ShapeDtypeStruct((R, n * D), F32),
        compiler_params=_cparams(("parallel", "parallel")),
        name="glu_tail",
    )(a2, w, g.reshape(1, D), x2)


def _glu_tail_prompt_kernel(z_ref, w_ref, g_ref, x_ref, o_ref, slab_ref, *, rk, ns):
    D = D_MODEL
    for s0 in range(0, CHUNK, ns):
        a = jnp.concatenate([z_ref[:, s * D:(s + 1) * D] for s in range(s0, s0 + ns)], axis=0)
        y = _dot(a, w_ref[...])
        y = y[:, :D] * jax.nn.sigmoid(y[:, D:])
        y = y * _inv_rms(y) * g_ref[...]
        for j in range(ns):
            for c in range(LANE_SLABS):
                slab_ref[c, pl.ds(s0 + j, rk, stride=CHUNK), :] = y[j * rk:(j + 1) * rk, c * LANES:(c + 1) * LANES]
    for c in range(LANE_SLABS):
        o_ref[:, c * LANES:(c + 1) * LANES] = x_ref[:, c * LANES:(c + 1) * LANES] + slab_ref[c]


def _glu_tail_prompt(z2, w, g, x2, *, rk, ns):
    R = z2.shape[0]
    D = D_MODEL
    assert R % rk == 0 and CHUNK % ns == 0 and x2.shape == (R * CHUNK, D)
    return pl.pallas_call(
        functools.partial(_glu_tail_prompt_kernel, rk=rk, ns=ns),
        grid=(R // rk,),
        in_specs=[pl.BlockSpec((rk, CHUNK * D), lambda i: (i, 0)),
                  _resident(w.shape),
                  pl.BlockSpec((1, D), lambda i: (0, 0)),
                  pl.BlockSpec((rk * CHUNK, D), lambda i: (i, 0))],
        out_specs=pl.BlockSpec((rk * CHUNK, D), lambda i: (i, 0)),
        out_shape=jax.ShapeDtypeStruct((R * CHUNK, D), F32),
        scratch_shapes=[pltpu.VMEM((LANE_SLABS, rk * CHUNK, LANES), F32)],
        compiler_params=_cparams(("parallel",)),
        name="glu_tail_prompt",
    )(z2, w, g.reshape(1, D), x2)


def _kvq_kernel(x_ref, gkv_ref, gq_ref, wk_ref, wv_ref, wq_ref, k_ref, v_ref, *q_refs, by_head, q_scale):
    x = x_ref[...]
    xr = x * _inv_rms(x)
    kv_in = (xr * gkv_ref[...]).astype(BF16)
    xn = (xr * gq_ref[...]).astype(BF16)
    k = _dot(kv_in, wk_ref[...])
    v = _dot(kv_in, wv_ref[...])
    q = _dot(xn, wq_ref[...]) * q_scale
    if by_head:
        for ref, val in ((k_ref, k), (v_ref, v), (q_refs[0], q)):
            for h in range(N_HEADS):
                ref[:, h, :] = val[:, h * V_DIM:(h + 1) * V_DIM]
    else:
        kb_ref, vb_ref, qb_ref = q_refs
        k_ref[...] = k
        v_ref[...] = v
        kb_ref[...] = k.astype(BF16)
        vb_ref[...] = v.astype(BF16)
        qb_ref[...] = q.astype(BF16)


def _kvq(x2, g_kv, g_q, wk, wv, wq, *, tm, by_head, q_scale):
    R, D = x2.shape
    tm = min(tm, R)
    assert R % tm == 0
    tile = pl.BlockSpec((tm, D), lambda i: (i, 0))
    heads = pl.BlockSpec((tm, N_HEADS, V_DIM), lambda i: (i, 0, 0))
    vec = pl.BlockSpec((1, D), lambda i: (0, 0))
    head32 = jax.ShapeDtypeStruct((R, N_HEADS, V_DIM), F32)
    flat32 = jax.ShapeDtypeStruct((R, D), F32)
    flat16 = jax.ShapeDtypeStruct((R, D), BF16)
    return pl.pallas_call(
        functools.partial(_kvq_kernel, by_head=by_head, q_scale=q_scale),
        grid=(R // tm,),
        in_specs=[tile, vec, vec, _resident(wk.shape), _resident(wv.shape), _resident(wq.shape)],
        out_specs=[heads] * 3 if by_head else [tile] * 5,
        out_shape=[head32] * 3 if by_head else [flat32, flat32, flat16, flat16, flat16],
        compiler_params=_cparams(("parallel",)),
        name="kvq_proj",
    )(x2, g_kv.reshape(1, D), g_q.reshape(1, D), wk, wv, wq)


FF_CHUNK = D_FF // 2


def _ffn_cols(c):
    return (slice(c * FF_CHUNK, (c + 1) * FF_CHUNK),
            slice(D_FF + c * FF_CHUNK, D_FF + (c + 1) * FF_CHUNK))


def _attn_residual(x, a, wo_ref, g_ref):
    y = _dot(a, wo_ref[...])
    return x + y * _inv_rms(y) * g_ref[...]


def _ffn_prompt_kernel(x_ref, *refs, tm, attn):
    attn_refs, refs = (refs[:3], refs[3:]) if attn else ((), refs)
    gpre_ref, gpost_ref, wup_ref, cw_ref, cb_ref, wdn_ref, o_ref, cs_ref, carry_ref = refs

    @pl.when(pl.program_id(1) == 0)
    def _():
        carry_ref[...] = jnp.zeros_like(carry_ref)

    x = x_ref[0]
    if attn:
        x = _attn_residual(x, attn_refs[0][0], attn_refs[1], attn_refs[2])
    xn = (x * _inv_rms(x) * gpre_ref[...]).astype(BF16)
    rows = lax.broadcasted_iota(jnp.int32, (tm, 1), 0)
    f = jnp.zeros((tm, D_MODEL), F32)
    for c in range(D_FF // FF_CHUNK):
        halves = []
        for cols in _ffn_cols(c):
            h = _dot(xn, wup_ref[:, cols])
            c0, c1 = carry_ref[6:7, cols], carry_ref[7:8, cols]
            h1 = jnp.where(rows == 0, c1, pltpu.roll(h, 1, axis=0))
            h2 = jnp.where(rows == 0, c0, jnp.where(rows == 1, c1, pltpu.roll(h, 2, axis=0)))
            halves.append(cb_ref[:, cols] + cw_ref[2:3, cols] * h
                          + cw_ref[1:2, cols] * h1 + cw_ref[0:1, cols] * h2)
            carry_ref[:, cols] = h[tm - 8:tm]
            cs_ref[0, :, cols] = h[tm - 2:tm]
        a = (_gelu(halves[0]) * halves[1]).astype(BF16)
        f = f + _dot(a, wdn_ref[c * FF_CHUNK:(c + 1) * FF_CHUNK, :])
    o_ref[0] = x + f * _inv_rms(f) * gpost_ref[...]


def _ffn_prompt(x, gpre, gpost, wup, cw, cb, wdn, *, tm, attn=None):
    B, L, D = x.shape
    tm = min(tm, L)
    assert L % tm == 0
    F2 = 2 * D_FF
    vec = pl.BlockSpec((1, D), lambda b, t: (0, 0))
    tile = pl.BlockSpec((1, tm, D), lambda b, t: (b, t, 0))
    attn_specs, attn_args = [], []
    if attn is not None:
        a, wo, g = attn
        attn_specs = [pl.BlockSpec((1, tm, a.shape[-1]), lambda b, t: (b, t, 0)), _resident(wo.shape), vec]
        attn_args = [a, wo, g.reshape(1, D)]
    return pl.pallas_call(
        functools.partial(_ffn_prompt_kernel, tm=tm, attn=attn is not None),
        grid=(B, L // tm),
        in_specs=[tile] + attn_specs + [vec, vec,
                  _resident(wup.shape), _resident((CONV_W, F2)), _resident((1, F2)),
                  _resident(wdn.shape)],
        out_specs=[pl.BlockSpec((1, tm, D), lambda b, t: (b, t, 0)),
                   pl.BlockSpec((1, CONV_W - 1, F2), lambda b, t: (b, 0, 0))],
        out_shape=[jax.ShapeDtypeStruct((B, L, D), F32),
                   jax.ShapeDtypeStruct((B, CONV_W - 1, F2), F32)],
        scratch_shapes=[pltpu.VMEM((8, F2), F32)],
        compiler_params=_cparams(("arbitrary", "arbitrary")),
        name="ffn_prompt",
    )(x, *attn_args, gpre.reshape(1, D), gpost.reshape(1, D), wup, cw, cb.reshape(1, F2), wdn)


def _ffn_sample_kernel(x_ref, buf_ref, *refs, tm, ntok, attn):
    attn_refs, refs = (refs[:3], refs[3:]) if attn else ((), refs)
    gpre_ref, gpost_ref, wup_ref, cw_ref, cb_ref, wdn_ref, o_ref, cs_ref = refs
    nsq = tm // ntok
    x = x_ref[...]
    if attn:
        x = _attn_residual(x, attn_refs[0][...], attn_refs[1], attn_refs[2])
    xn = (x * _inv_rms(x) * gpre_ref[...]).astype(BF16)
    sub = lax.broadcasted_iota(jnp.int32, (tm, 1), 0) % ntok
    f = jnp.zeros((tm, D_MODEL), F32)
    for c in range(D_FF // FF_CHUNK):
        halves = []
        for cols in _ffn_cols(c):
            h = _dot(xn, wup_ref[:, cols])
            spread = lambda b: jnp.broadcast_to(b, (nsq, ntok, FF_CHUNK)).reshape(tm, FF_CHUNK)
            b0, b1 = spread(buf_ref[:, 0:1, cols]), spread(buf_ref[:, 1:2, cols])
            h1 = jnp.where(sub == 0, b1, pltpu.roll(h, 1, axis=0))
            h2 = jnp.where(sub == 0, b0, jnp.where(sub == 1, b1, pltpu.roll(h, 2, axis=0)))
            halves.append(cb_ref[:, cols] + cw_ref[2:3, cols] * h
                          + cw_ref[1:2, cols] * h1 + cw_ref[0:1, cols] * h2)
            cs_ref[:, :, cols] = h.reshape(nsq, ntok, FF_CHUNK)[:, ntok - 2:ntok, :]
        a = (_gelu(halves[0]) * halves[1]).astype(BF16)
        f = f + _dot(a, wdn_ref[c * FF_CHUNK:(c + 1) * FF_CHUNK, :])
    o_ref[...] = x + f * _inv_rms(f) * gpost_ref[...]


def _ffn_sample(x, buf, gpre, gpost, wup, cw, cb, wdn, *, tm, attn=None):
    S, T, D = x.shape
    assert T == 8 and tm % T == 0 and (S * T) % tm == 0
    F2 = 2 * D_FF
    nsq = tm // T
    vec = pl.BlockSpec((1, D), lambda i: (0, 0))
    attn_specs, attn_args = [], []
    if attn is not None:
        a, wo, g = attn
        attn_specs = [pl.BlockSpec((tm, a.shape[-1]), lambda i: (i, 0)), _resident(wo.shape), vec]
        attn_args = [a, wo, g.reshape(1, D)]
    out, cs = pl.pallas_call(
        functools.partial(_ffn_sample_kernel, tm=tm, ntok=T, attn=attn is not None),
        grid=(S * T // tm,),
        in_specs=[pl.BlockSpec((tm, D), lambda i: (i, 0)),
                  pl.BlockSpec((nsq, CONV_W - 1, F2), lambda i: (i, 0, 0))] + attn_specs + [vec, vec,
                  _resident(wup.shape), _resident((CONV_W, F2)), _resident((1, F2)),
                  _resident(wdn.shape)],
        out_specs=[pl.BlockSpec((tm, D), lambda i: (i, 0)),
                   pl.BlockSpec((nsq, CONV_W - 1, F2), lambda i: (i, 0, 0))],
        out_shape=[jax.ShapeDtypeStruct((S * T, D), F32),
                   jax.ShapeDtypeStruct((S, CONV_W - 1, F2), F32)],
        compiler_params=_cparams(("parallel",)),
        name="ffn_sample",
    )(x.reshape(S * T, D), buf, *attn_args, gpre.reshape(1, D), gpost.reshape(1, D), wup, cw,
      cb.reshape(1, F2), wdn)
    return out.reshape(S, T, D), cs


def _lam_value(lq1, lk1, lq2, lk2, lam_init):
    return (jnp.exp(jnp.sum(lq1[...] * lk1[...], axis=-1, keepdims=True))
            - jnp.exp(jnp.sum(lq2[...] * lk2[...], axis=-1, keepdims=True)) + lam_init)


def _sub_norm(o, subg, lam_init):
    return o * _inv_rms(o) * subg * (1.0 - lam_init)


LOG2E = math.log2(math.e)
FLASH_ROWS = 256
FLASH_AHEAD = 1
POS_SPLIT = 64


def _flash_kernel(qi_ref, ki_ref, q_ref, k_ref, v_ref, lq1, lk1, lq2, lk2, subg_ref, o_ref,
                  q2_ref, kf_ref, m_ref, acc_ref, *, tq, lam_init):
    h, step = pl.program_id(1), pl.program_id(2)
    qi, ki = qi_ref[step], ki_ref[step]
    RQ = FLASH_ROWS
    n_chunks = tq // RQ
    lane = lax.broadcasted_iota(jnp.int32, (1, V_DIM), 1)
    slope2 = jnp.exp2(-jnp.full((1, V_DIM), h + 1, jnp.int32).astype(F32)) * LOG2E

    @pl.when(ki == 0)
    def _():
        m_ref[...] = jnp.full_like(m_ref, -jnp.inf)
        acc_ref[...] = jnp.zeros_like(acc_ref)
        kidx = lax.broadcasted_iota(jnp.int32, (tq, 1), 0)
        a, b = (kidx // POS_SPLIT).astype(F32), (kidx % POS_SPLIT).astype(F32)
        kf_ref[...] = jnp.where(lane < 3, a, jnp.where(lane < 6, b, 0.0)).astype(BF16)
        c0 = slope2.astype(BF16).astype(F32)
        c1 = (slope2 - c0).astype(BF16).astype(F32)
        c2 = (slope2 - c0 - c1).astype(BF16).astype(F32)
        part = jnp.where(lane % 3 == 0, c0, jnp.where(lane % 3 == 1, c1, c2))
        qf = jnp.where(lane < 3, POS_SPLIT * part, jnp.where(lane < 6, part, 0.0)).astype(BF16)
        qf = jnp.broadcast_to(qf, (RQ, V_DIM))
        for c_i in range(n_chunks):
            q = q_ref[c_i * RQ:(c_i + 1) * RQ, :]
            zero = jnp.zeros_like(q)
            q2_ref[c_i, 0:RQ, 0:V_DIM] = jnp.where(lane < HEAD_DIM, q, zero)
            q2_ref[c_i, RQ:2 * RQ, 0:V_DIM] = jnp.where(lane >= HEAD_DIM, q, zero)
            q2_ref[c_i, 0:RQ, V_DIM:] = qf
            q2_ref[c_i, RQ:2 * RQ, V_DIM:] = qf

    def absorb(diagonal):
        k_aug = jnp.concatenate([k_ref[...], kf_ref[...]], axis=1)
        v_aug = jnp.concatenate([v_ref[...], jnp.ones((tq, V_DIM), BF16)], axis=1)
        shift = slope2 * ((ki - qi) * tq).astype(F32)
        n_keys = lambda c: (c + 1) * RQ if diagonal else tq
        scores = lambda c: lax.dot_general(q2_ref[c], k_aug[0:n_keys(c)], (((1,), (1,)), ((), ())),
                                           preferred_element_type=F32)
        ahead = [scores(c) for c in range(min(FLASH_AHEAD, n_chunks))]
        for c_i in range(n_chunks):
            nkeys = n_keys(c_i)
            s = ahead.pop(0)
            if c_i + FLASH_AHEAD < n_chunks:
                ahead.append(scores(c_i + FLASH_AHEAD))
            if diagonal:
                kcol = lax.broadcasted_iota(jnp.int32, (1, RQ), 1)
                qrow = lax.broadcasted_iota(jnp.int32, (2 * RQ, 1), 0) % RQ
                tail = jnp.where(kcol <= qrow, s[:, nkeys - RQ:], -jnp.inf)
                s = tail if nkeys == RQ else jnp.concatenate([s[:, 0:nkeys - RQ], tail], axis=1)
            m_old = m_ref[c_i]
            m_new = jnp.maximum(m_old, jnp.max(s, axis=-1, keepdims=True) + shift)
            alpha = jnp.exp2(m_old - m_new)
            p = jnp.exp2((s - jnp.concatenate([m_new - shift] * (nkeys // V_DIM), axis=1)).astype(BF16))
            pv = _dot(p, v_aug[0:nkeys])
            acc_ref[c_i] = jnp.concatenate([alpha, alpha], axis=1) * acc_ref[c_i] + pv
            m_ref[c_i] = m_new

    @pl.when(ki < qi)
    def _():
        absorb(False)

    @pl.when(ki == qi)
    def _():
        absorb(True)
        lam = _lam_value(lq1, lk1, lq2, lk2, lam_init)
        for c_i in range(n_chunks):
            acc = acc_ref[c_i]
            o = (acc[0:RQ, 0:V_DIM] / acc[0:RQ, V_DIM:]
                 - lam * (acc[RQ:, 0:V_DIM] / acc[RQ:, V_DIM:]))
            o_ref[c_i * RQ:(c_i + 1) * RQ, :] = _sub_norm(o, subg_ref[...], lam_init).astype(o_ref.dtype)


def _flash_prompt(qb, kb, vb, lq1, lk1, lq2, lk2, subg, *, B, L, tq, lam_init):
    tq = min(tq, L)
    assert L % tq == 0 and tq % FLASH_ROWS == 0 and tq <= POS_SPLIT * POS_SPLIT
    nq = L // tq
    n_chunks = tq // FLASH_ROWS
    pairs = [(i, j) for i in range(nq) for j in range(i + 1)]
    qi_tab = jnp.asarray([p[0] for p in pairs], jnp.int32)
    ki_tab = jnp.asarray([p[1] for p in pairs], jnp.int32)
    lvec = pl.BlockSpec((1, HEAD_DIM), lambda b, h, s, qt, kt: (0, 0))
    grid_spec = pltpu.PrefetchScalarGridSpec(
        num_scalar_prefetch=2,
        grid=(B, N_HEADS, len(pairs)),
        in_specs=[pl.BlockSpec((tq, V_DIM), lambda b, h, s, qt, kt: (b * nq + qt[s], h)),
                  pl.BlockSpec((tq, V_DIM), lambda b, h, s, qt, kt: (b * nq + kt[s], h)),
                  pl.BlockSpec((tq, V_DIM), lambda b, h, s, qt, kt: (b * nq + kt[s], h)),
                  lvec, lvec, lvec, lvec,
                  pl.BlockSpec((1, V_DIM), lambda b, h, s, qt, kt: (0, 0))],
        out_specs=pl.BlockSpec((tq, V_DIM), lambda b, h, s, qt, kt: (b * nq + qt[s], h)),
        scratch_shapes=[pltpu.VMEM((n_chunks, 2 * FLASH_ROWS, 2 * V_DIM), BF16),
                        pltpu.VMEM((tq, V_DIM), BF16),
                        pltpu.VMEM((n_chunks, 2 * FLASH_ROWS, V_DIM), F32),
                        pltpu.VMEM((n_chunks, 2 * FLASH_ROWS, 2 * V_DIM), F32)])
    return pl.pallas_call(
        functools.partial(_flash_kernel, tq=tq, lam_init=lam_init),
        grid_spec=grid_spec,
        out_shape=jax.ShapeDtypeStruct((B * L, N_HEADS * V_DIM), BF16),
        compiler_params=_cparams(("parallel", "parallel", "arbitrary")),
        name="flash_prompt",
    )(qi_tab, ki_tab, qb, kb, vb, lq1, lk1, lq2, lk2, subg)


def _paged_kernel(pt_ref, q_ref, *refs, ntok, page, npp, past_len, lam_init):
    kc = refs[:npp]
    vc = refs[npp:2 * npp]
    (kn_ref, vn_ref, lq1, lk1, lq2, lk2, subg_ref, o_ref,
     qall_ref, bias_ref, m_ref, l_ref, acc_ref) = refs[2 * npp:]
    H = N_HEADS
    R = 2 * ntok * H
    PW = page * H
    pg = pl.program_id(1)

    row = lax.broadcasted_iota(jnp.int32, (R, 1), 0)
    slope_r = jnp.exp2(-(row % H + 1).astype(F32)) * LOG2E
    qi_r = (row // H) % ntok

    def head_bias(n_lanes, key_limit):
        lane = lax.broadcasted_iota(jnp.int32, (1, n_lanes), 1)
        key = lane // H
        ok = (lane % H == row % H) & (key < key_limit)
        return jnp.where(ok, slope_r * (key - qi_r).astype(F32), -jnp.inf)

    @pl.when(pg == 0)
    def _():
        m_ref[...] = jnp.full_like(m_ref, -jnp.inf)
        l_ref[...] = jnp.zeros_like(l_ref)
        acc_ref[...] = jnp.zeros_like(acc_ref)
        q2 = q_ref[...].reshape(ntok * H, V_DIM)
        col = lax.broadcasted_iota(jnp.int32, (1, V_DIM), 1)
        qall_ref[...] = jnp.concatenate([jnp.where(col < HEAD_DIM, q2, 0.0),
                                         jnp.where(col >= HEAD_DIM, q2, 0.0)], axis=0).astype(BF16)
        bias_ref[...] = head_bias(PW, page)

    def absorb(ks, vs, bias, shifts):
        qall = qall_ref[...]
        s = [lax.dot_general(qall, k, (((1,), (1,)), ((), ())), preferred_element_type=F32) + bias
             for k in ks]
        m_old = m_ref[...]
        m_new = m_old
        for si, sh in zip(s, shifts):
            m_new = jnp.maximum(m_new, jnp.max(si, axis=-1, keepdims=True) + sh)
        alpha = jnp.exp2(m_old - m_new)
        l_new = alpha * l_ref[...]
        acc = alpha * acc_ref[...]
        for si, sh, v in zip(s, shifts, vs):
            p = jnp.exp2(si - (m_new - sh))
            l_new = l_new + jnp.sum(p, axis=-1, keepdims=True)
            acc = acc + _dot(p.astype(BF16), v)
        m_ref[...] = m_new
        l_ref[...] = l_new
        acc_ref[...] = acc

    flat = lambda ref: ref[0].reshape(PW, V_DIM).astype(BF16)
    shifts = [slope_r * ((pg * npp + i) * page - past_len).astype(F32) for i in range(npp)]
    absorb([flat(r) for r in kc], [flat(r) for r in vc], bias_ref[...], shifts)

    @pl.when(pg == pl.num_programs(1) - 1)
    def _():
        n_new = ntok * H
        pad = jnp.zeros((R - n_new, V_DIM), F32)
        kn = jnp.concatenate([kn_ref[...].reshape(n_new, V_DIM), pad], axis=0).astype(BF16)
        vn = jnp.concatenate([vn_ref[...].reshape(n_new, V_DIM), pad], axis=0).astype(BF16)
        absorb([kn], [vn], head_bias(R, jnp.minimum(qi_r + 1, ntok)), [0.0])
        lam = _lam_value(lq1, lk1, lq2, lk2, lam_init)
        h = R // 2
        o = acc_ref[0:h] / l_ref[0:h] - lam * (acc_ref[h:R] / l_ref[h:R])
        o_ref[...] = _sub_norm(o, subg_ref[...], lam_init).reshape(ntok, H, V_DIM)


def _paged_sample(page_table, q, cache_k, cache_v, k_new, v_new, lq1, lk1, lq2, lk2, subg,
                  *, ntok, npp, lam_init):
    S, n_pages = page_table.shape
    page = cache_k.shape[1]
    H = N_HEADS
    R = 2 * ntok * H
    assert n_pages % npp == 0
    tok = pl.BlockSpec((ntok, H, V_DIM), lambda s, p, pt: (s, 0, 0))
    pgs = [pl.BlockSpec((1, page, H, V_DIM), functools.partial(
        lambda s, p, pt, i: (pt[s, p * npp + i], 0, 0, 0), i=i)) for i in range(npp)]
    lvec = pl.BlockSpec((1, HEAD_DIM), lambda s, p, pt: (0, 0))
    grid_spec = pltpu.PrefetchScalarGridSpec(
        num_scalar_prefetch=1,
        grid=(S, n_pages // npp),
        in_specs=[tok] + pgs + pgs + [tok, tok, lvec, lvec, lvec, lvec,
                                      pl.BlockSpec((1, V_DIM), lambda s, p, pt: (0, 0))],
        out_specs=tok,
        scratch_shapes=[pltpu.VMEM((R, V_DIM), BF16), pltpu.VMEM((R, page * H), F32),
                        pltpu.VMEM((R, 1), F32), pltpu.VMEM((R, 1), F32), pltpu.VMEM((R, V_DIM), F32)])
    return pl.pallas_call(
        functools.partial(_paged_kernel, ntok=ntok, page=page, npp=npp, past_len=n_pages * page,
                          lam_init=lam_init),
        grid_spec=grid_spec,
        out_shape=jax.ShapeDtypeStruct((S * ntok, H, V_DIM), F32),
        compiler_params=_cparams(("arbitrary", "arbitrary")),
        name="paged_sample",
    )(page_table, q, *([cache_k] * npp), *([cache_v] * npp), k_new, v_new, lq1, lk1, lq2, lk2, subg)


def kernel(x_prompt, x_sample, state_ssm_re, state_ssm_im, state_conv, cache_k, cache_v, page_table,
           a_pre_g, a_post_g, ssm_lam_re, ssm_lam_im, ssm_log_dt, ssm_b_re, ssm_b_im, ssm_c_re,
           ssm_c_im, ssm_d, glu_w, kv_norm_g, w_k, w_v, b_pre_g, b_post_g, w_q, lam_q1, lam_k1,
           lam_q2, lam_k2, sub_g, w_o, f_pre_g, f_post_g, w_up, conv_w, conv_b, w_down):
    B, L, D = x_prompt.shape
    S, T, _ = x_sample.shape
    P, G = SSM_STATE, N_GROUPS
    HW = N_HEADS * V_DIM
    st_dtype = state_ssm_re.dtype

    wt, tm_op, acol = _s5_prepare(ssm_lam_re[0], ssm_lam_im[0], ssm_log_dt[0], ssm_b_re[0],
                                  ssm_b_im[0], ssm_c_re[0], ssm_c_im[0])
    glu_b = glu_w[0].astype(BF16)
    zp, hp = _s5_mix_prompt(x_prompt, a_pre_g[0], ssm_d[0], wt, tm_op, acol)
    xp = _glu_tail_prompt(zp.reshape(B * L // CHUNK, CHUNK * D), glu_b, a_post_g[0],
                          x_prompt.reshape(B * L, D), rk=64, ns=8).reshape(B, L, D)
    h0 = jnp.concatenate([state_ssm_re[0].astype(F32), state_ssm_im[0].astype(F32)], axis=-1)
    zs, hs = _s5_mix_sample(x_sample, h0.transpose(1, 2, 0), a_pre_g[0], ssm_d[0], wt, tm_op, acol)
    xs = _glu_tail(zs, glu_b, a_post_g[0], x_sample.reshape(S, T * D), tm=512).reshape(S, T, D)

    hp = hp.reshape(B, G, 2 * P)
    hs = hs.transpose(2, 0, 1)
    ssm_re_p, ssm_im_p = hp[None, ..., :P].astype(st_dtype), hp[None, ..., P:].astype(st_dtype)
    ssm_re_s, ssm_im_s = hs[None, ..., :P].astype(st_dtype), hs[None, ..., P:].astype(st_dtype)

    wup_b, wdn_b = w_up.astype(BF16), w_down.astype(BF16)
    xp, conv_p0 = _ffn_prompt(xp, f_pre_g[0], f_post_g[0], wup_b[0], conv_w[0], conv_b[0], wdn_b[0], tm=512)
    xs, conv_s0 = _ffn_sample(xs, state_conv[0], f_pre_g[0], f_post_g[0], wup_b[0], conv_w[0],
                              conv_b[0], wdn_b[0], tm=512)

    lam_init = 0.8 - 0.6 * math.exp(-0.3 * N_A_LAYERS)
    wk_b, wv_b, wq_b, wo_b = w_k.astype(BF16), w_v.astype(BF16), w_q[0].astype(BF16), w_o[0].astype(BF16)
    lvec = lambda a: a[0].reshape(1, HEAD_DIM).astype(F32)
    lams = (lvec(lam_q1), lvec(lam_k1), lvec(lam_q2), lvec(lam_k2))
    subg = sub_g[0].reshape(1, V_DIM)

    xp2 = xp.reshape(B * L, D)
    kp, vp, kpb, vpb, qpb = _kvq(xp2, kv_norm_g, b_pre_g[0], wk_b, wv_b, wq_b, tm=512, by_head=False,
                                 q_scale=LOG2E * HEAD_DIM ** -0.5)
    op = _flash_prompt(qpb, kpb, vpb, *lams, subg, B=B, L=L, tq=2048, lam_init=lam_init)

    xs2 = xs.reshape(S * T, D)
    ks, vs, qs = _kvq(xs2, kv_norm_g, b_pre_g[0], wk_b, wv_b, wq_b, tm=512, by_head=True,
                      q_scale=LOG2E * HEAD_DIM ** -0.5)
    os_ = _paged_sample(page_table, qs, cache_k, cache_v, ks, vs, *lams, subg, ntok=T, npp=16,
                        lam_init=lam_init)

    xp, conv_p1 = _ffn_prompt(xp, f_pre_g[1], f_post_g[1], wup_b[1], conv_w[1], conv_b[1], wdn_b[1], tm=512,
                              attn=(op.reshape(B, L, HW), wo_b, b_post_g[0]))
    xs, conv_s1 = _ffn_sample(xs, state_conv[1], f_pre_g[1], f_post_g[1], wup_b[1], conv_w[1],
                              conv_b[1], wdn_b[1], tm=256,
                              attn=(os_.reshape(S * T, HW).astype(BF16), wo_b, b_post_g[0]))

    return (xp, xs, ssm_re_p, ssm_im_p, jnp.stack([conv_p0, conv_p1]),
            kp.reshape(B, L, N_HEADS, 2 * HEAD_DIM), vp.reshape(B, L, N_HEADS, V_DIM),
            ssm_re_s, ssm_im_s, jnp.stack([conv_s0, conv_s1]),
            ks.reshape(S, T, N_HEADS, 2 * HEAD_DIM), vs.reshape(S, T, N_HEADS, V_DIM))
```
